```python
import jax, jax.numpy as jnp
from jax import lax
import numpy as np

D_MODEL = 2048
BATCH = 4
SEQ = 2048
DEPTH = 2
DEC_BATCH = 32
DEC_SEQ = 32
PAST_LEN = 2048

CHUNK = 64
N_AB_LAYERS = (DEPTH + 1) // 2
N_ATT_LAYERS = DEPTH // 2
RMS_EPS = 1e-6

MIX_WIDTH = D_MODEL
H_RET = 4
DV_RET = MIX_WIDTH // 2 // H_RET
DK_RET = DV_RET
H_GLA = 4
DV_GLA = MIX_WIDTH // 2 // H_GLA
DK_GLA = DV_GLA // 2
GLA_GATE_RANK = 16
GLA_GATE_TAU = 16.0
ROPE_BASE = 10000.0
AB_SPLIT = (H_RET * DK_RET, H_RET * DK_RET, H_RET * DV_RET, H_RET * DV_RET,
            H_GLA * DK_GLA, H_GLA * DK_GLA, H_GLA * DV_GLA, H_GLA * DV_GLA, GLA_GATE_RANK)
AB_IN_WIDTH = 2 * H_RET * DK_RET + 2 * H_RET * DV_RET + 2 * H_GLA * DK_GLA + 2 * H_GLA * DV_GLA + GLA_GATE_RANK

H_ATT = 16
DH_ATT = MIX_WIDTH // H_ATT
LEFT_CHUNKS = 8
MAX_REL = 256
NEG_INF = -1e30

D_FF = 5632
CONV_W = 3

kernel_name = "hybrid_streaming_retention_gla_chunkattn_convffn_step"


def rmsnorm(x, g):
    xf = x.astype(jnp.float32)
    y = xf * lax.rsqrt(jnp.mean(xf * xf, axis=-1, keepdims=True) + RMS_EPS)
    return (y * g.astype(jnp.float32)).astype(x.dtype)


def head_rmsnorm(o, g):
    return o * lax.rsqrt(jnp.mean(o * o, axis=-1, keepdims=True) + RMS_EPS) * g.astype(jnp.float32)


def rope(x, pos):
    half = x.shape[-1] // 2
    inv = ROPE_BASE ** (-jnp.arange(half, dtype=jnp.float32) / half)
    ang = pos.astype(jnp.float32)[:, None] * inv[None, :]
    cos = jnp.cos(ang)[None, :, None, :]
    sin = jnp.sin(ang)[None, :, None, :]
    x1, x2 = x[..., :half], x[..., half:]
    return jnp.concatenate([x1 * cos - x2 * sin, x1 * sin + x2 * cos], axis=-1)


def retention_chunk(S, q, k, v, log_gamma):
    C = q.shape[1]
    idx = jnp.arange(C, dtype=jnp.float32)
    diff = idx[:, None] - idx[None, :]
    causal = (diff >= 0)[None]
    decay = jnp.where(causal, jnp.exp(jnp.where(causal, diff[None], 0.0) * log_gamma[:, None, None]), 0.0)
    scores = jnp.einsum('bihd,bjhd->bhij', q, k) * decay[None]
    o = jnp.einsum('bhij,bjhv->bihv', scores, v)
    q_dec = jnp.exp((idx + 1.0)[None, :] * log_gamma[:, None])
    o = o + jnp.einsum('bihd,hi,bhdv->bihv', q, q_dec, S)
    k_dec = jnp.exp((C - 1.0 - idx)[None, :] * log_gamma[:, None])
    S_new = jnp.exp(C * log_gamma)[None, :, None, None] * S + jnp.einsum('bjhd,hj,bjhv->bhdv', k, k_dec, v)
    return S_new, o


def gla_chunk(S, q, k, v, log_a):
    C = q.shape[1]
    b = jnp.cumsum(log_a, axis=1)
    causal = jnp.tril(jnp.ones((C, C), dtype=bool))[None, :, :, None, None]
    diff = b[:, :, None] - b[:, None, :]
    w = jnp.where(causal, jnp.exp(jnp.where(causal, diff, 0.0)), 0.0)
    scores = jnp.einsum('bihd,bjhd,bijhd->bhij', q, k, w)
    o = jnp.einsum('bhij,bjhv->bihv', scores, v)
    o = o + jnp.einsum('bihd,bhdv->bihv', q * jnp.exp(b), S)
    b_last = b[:, -1]
    S_new = jnp.exp(b_last)[..., None] * S + jnp.einsum('bjhd,bjhv->bhdv', k * jnp.exp(b_last[:, None] - b), v)
    return S_new, o


def chunk_scan(step, S0, *xs):
    def to_chunks(a):
        B, T = a.shape[:2]
        return jnp.moveaxis(a.reshape((B, T // CHUNK, CHUNK) + a.shape[2:]), 1, 0)
    S, o = lax.scan(lambda S, c: step(S, *c), S0, tuple(to_chunks(a) for a in xs))
    o = jnp.moveaxis(o, 0, 1)
    return S, o.reshape((o.shape[0], -1) + o.shape[3:])


def ab_mixer(h, pos, s_ret, s_gla, w_in, gate_w2, gate_b, ret_g, gla_g, w_out):
    B, T, _ = h.shape
    f32 = jnp.float32
    z = jnp.einsum('btd,de->bte', h, w_in).astype(f32)
    cuts = [int(c) for c in np.cumsum(AB_SPLIT)[:-1]]
    qa, ka, va, ga, qb, kb, vb, gb, lo = jnp.split(z, cuts, axis=-1)
    qa = rope(qa.reshape(B, T, H_RET, DK_RET), pos)
    ka = rope(ka.reshape(B, T, H_RET, DK_RET), pos) * (DK_RET ** -0.5)
    va = va.reshape(B, T, H_RET, DV_RET)
    log_gamma = jnp.log1p(-jnp.exp2(-5.0 - jnp.arange(H_RET, dtype=f32)))
    qb = qb.reshape(B, T, H_GLA, DK_GLA) * (DK_GLA ** -0.5)
    kb = kb.reshape(B, T, H_GLA, DK_GLA)
    vb = vb.reshape(B, T, H_GLA, DV_GLA)
    log_a = (jax.nn.log_sigmoid(lo @ gate_w2.astype(f32) + gate_b.astype(f32)) / GLA_GATE_TAU).reshape(B, T, H_GLA, DK_GLA)
    if s_ret is None:
        s_ret0 = jnp.zeros((B, H_RET, DK_RET, DV_RET), f32)
        s_gla0 = jnp.zeros((B, H_GLA, DK_GLA, DV_GLA), f32)
        s_ret_new, o_ret = chunk_scan(lambda S, q, k, v: retention_chunk(S, q, k, v, log_gamma), s_ret0, qa, ka, va)
        s_gla_new, o_gla = chunk_scan(gla_chunk, s_gla0, qb, kb, vb, log_a)
    else:
        s_ret_new, o_ret = retention_chunk(s_ret.astype(f32), qa, ka, va, log_gamma)
        s_gla_new, o_gla = gla_chunk(s_gla.astype(f32), qb, kb, vb, log_a)
    o_ret = head_rmsnorm(o_ret, ret_g).reshape(B, T, -1) * jax.nn.silu(ga)
    o_gla = head_rmsnorm(o_gla, gla_g).reshape(B, T, -1) * jax.nn.silu(gb)
    o = jnp.concatenate([o_ret, o_gla], axis=-1).astype(h.dtype)
    return jnp.einsum('bte,ed->btd', o, w_out), s_ret_new, s_gla_new


def rel_bias_lookup(rel_bias, rel):
    return rel_bias.astype(jnp.float32)[:, jnp.clip(rel, -MAX_REL, MAX_REL) + MAX_REL]


def band_attention_prompt(q, k, v, rel_bias):
    B, T, H, dh = q.shape
    nc = T // CHUNK
    pad = LEFT_CHUNKS * CHUNK
    band = pad + CHUNK
    kp = jnp.pad(k, ((0, 0), (pad, 0), (0, 0), (0, 0)))
    vp = jnp.pad(v, ((0, 0), (pad, 0), (0, 0), (0, 0)))
    qc = q.reshape(B, nc, CHUNK, H, dh)
    i = jnp.arange(CHUNK)
    j = jnp.arange(band)
    bias = rel_bias_lookup(rel_bias, (i[:, None] + pad) - j[None, :])
    scale = dh ** -0.5

    def one_chunk(n):
        ks = lax.dynamic_slice_in_dim(kp, n * CHUNK, band, axis=1)
        vs = lax.dynamic_slice_in_dim(vp, n * CHUNK, band, axis=1)
        qn = lax.dynamic_index_in_dim(qc, n, axis=1, keepdims=False)
        s = jnp.einsum('bihd,bjhd->bhij', qn, ks) * scale + bias[None]
        valid = (n * CHUNK - pad + j) >= 0
        s = jnp.where(valid[None, None, None, :], s, NEG_INF)
        p = jax.nn.softmax(s, axis=-1)
        return jnp.einsum('bhij,bjhd->bihd', p, vs)

    o = lax.map(one_chunk, jnp.arange(nc))
    return jnp.moveaxis(o, 0, 1).reshape(B, T, H, dh)


def band_attention_sample(q, k_new, v_new, k_cache, v_cache, rel_bias):
    T = q.shape[1]
    Wc = k_cache.shape[1]
    ks = jnp.concatenate([k_cache.astype(jnp.float32), k_new], axis=1)
    vs = jnp.concatenate([v_cache.astype(jnp.float32), v_new], axis=1)
    q_pos = PAST_LEN + jnp.arange(T)
    key_pos = jnp.concatenate([PAST_LEN - Wc + jnp.arange(Wc), PAST_LEN + jnp.arange(T)])
    bias = rel_bias_lookup(rel_bias, q_pos[:, None] - key_pos[None, :])
    s = jnp.einsum('bihd,bjhd->bhij', q, ks) * (q.shape[-1] ** -0.5) + bias[None]
    p = jax.nn.softmax(s, axis=-1)
    return jnp.einsum('bhij,bjhd->bihd', p, vs)


def c_mixer(h, k_cache, v_cache, w_qkv, rel_bias, w_out):
    B, T, _ = h.shape
    z = jnp.einsum('btd,de->bte', h, w_qkv).astype(jnp.float32)
    q, k, v = [a.reshape(B, T, H_ATT, DH_ATT) for a in jnp.split(z, 3, axis=-1)]
    if k_cache is None:
        o = band_attention_prompt(q, k, v, rel_bias)
        keep = min(LEFT_CHUNKS * CHUNK, T)
        new_k, new_v = k[:, T - keep:], v[:, T - keep:]
    else:
        o = band_attention_sample(q, k, v, k_cache, v_cache, rel_bias)
        new_k, new_v = k, v
    out = jnp.einsum('bte,ed->btd', o.reshape(B, T, -1).astype(h.dtype), w_out)
    return out, new_k, new_v


def conv_ffn(h, conv_buf, w_up, conv_w, conv_b, w_down):
    B, T, _ = h.shape
    up = jnp.einsum('btd,df->btf', h, w_up)
    g, u = jnp.split(up, 2, axis=-1)
    if conv_buf is None:
        hist = jnp.zeros((B, CONV_W - 1, D_FF), g.dtype)
    else:
        hist = conv_buf.astype(g.dtype)
    gp = jnp.concatenate([hist, g], axis=1)
    gc = conv_b
    for w in range(CONV_W):
        gc = gc + gp[:, w:w + T] * conv_w[w]
    act = jax.nn.silu(gc) * u
    return jnp.einsum('btf,fd->btd', act, w_down), gp[:, -(CONV_W - 1):]


def trunk(x, pos, ret_in, gla_in, k_in, v_in, conv_in, p):
    ret_out, gla_out, k_out, v_out, conv_out = [], [], [], [], []
    for l in range(DEPTH):
        h = rmsnorm(x, p['norm_mix_g'][l])
        if l % 2 == 0:
            a = l // 2
            mix, s_r, s_g = ab_mixer(h, pos,
                                     None if ret_in is None else ret_in[a],
                                     None if gla_in is None else gla_in[a],
                                     p['w_in_ab'][a], p['gla_gate_w2'][a], p['gla_gate_b'][a],
                                     p['ret_norm_g'][a], p['gla_norm_g'][a], p['w_out_ab'][a])
            ret_out.append(s_r.astype(x.dtype))
            gla_out.append(s_g.astype(x.dtype))
        else:
            c = l // 2
            mix, nk, nv = c_mixer(h,
                                  None if k_in is None else k_in[c],
                                  None if v_in is None else v_in[c],
                                  p['w_qkv_att'][c], p['rel_bias_att'][c], p['w_out_att'][c])
            k_out.append(nk.astype(x.dtype))
            v_out.append(nv.astype(x.dtype))
        x = x + mix
        h = rmsnorm(x, p['norm_ffn_g'][l])
        f, buf = conv_ffn(h, None if conv_in is None else conv_in[l],
                          p['w_ffn_up'][l], p['ffn_conv_w'][l], p['ffn_conv_b'][l], p['w_ffn_down'][l])
        conv_out.append(buf.astype(x.dtype))
        x = x + f
    y = rmsnorm(x, p['norm_final_g'])
    return y, jnp.stack(ret_out), jnp.stack(gla_out), jnp.stack(k_out), jnp.stack(v_out), jnp.stack(conv_out)


def setup_inputs(seed: int = 0) -> dict:
    key = jax.random.key(seed)
    ks = jax.random.split(key, 24)
    nrm = jax.random.normal
    f32 = jnp.float32
    att_cache = min(LEFT_CHUNKS * CHUNK, PAST_LEN)
    return {
        'x_prompt': nrm(ks[0], (BATCH, SEQ, D_MODEL), f32),
        'x_sample': nrm(ks[1], (DEC_BATCH, DEC_SEQ, D_MODEL), f32),
        'state_ret': nrm(ks[2], (N_AB_LAYERS, DEC_BATCH, H_RET, DK_RET, DV_RET), f32),
        'state_gla': nrm(ks[3], (N_AB_LAYERS, DEC_BATCH, H_GLA, DK_GLA, DV_GLA), f32),
        'cache_attn_k': nrm(ks[4], (N_ATT_LAYERS, DEC_BATCH, att_cache, H_ATT, DH_ATT), f32),
        'cache_attn_v': nrm(ks[5], (N_ATT_LAYERS, DEC_BATCH, att_cache, H_ATT, DH_ATT), f32),
        'state_ffn_conv': nrm(ks[6], (DEPTH, DEC_BATCH, CONV_W - 1, D_FF), f32),
        'norm_mix_g': 1.0 + 0.05 * nrm(ks[7], (DEPTH, D_MODEL), f32),
        'w_in_ab': nrm(ks[8], (N_AB_LAYERS, D_MODEL, AB_IN_WIDTH), f32) * D_MODEL ** -0.5,
        'gla_gate_w2': nrm(ks[9], (N_AB_LAYERS, GLA_GATE_RANK, H_GLA * DK_GLA), f32) * GLA_GATE_RANK ** -0.5,
        'gla_gate_b': 0.1 * nrm(ks[10], (N_AB_LAYERS, H_GLA * DK_GLA), f32),
        'ret_norm_g': 1.0 + 0.05 * nrm(ks[11], (N_AB_LAYERS, DV_RET), f32),
        'gla_norm_g': 1.0 + 0.05 * nrm(ks[12], (N_AB_LAYERS, DV_GLA), f32),
        'w_out_ab': nrm(ks[13], (N_AB_LAYERS, MIX_WIDTH, D_MODEL), f32) * MIX_WIDTH ** -0.5,
        'w_qkv_att': nrm(ks[14], (N_ATT_LAYERS, D_MODEL, 3 * MIX_WIDTH), f32) * D_MODEL ** -0.5,
        'rel_bias_att': 0.2 * nrm(ks[15], (N_ATT_LAYERS, H_ATT, 2 * MAX_REL + 1), f32),
        'w_out_att': nrm(ks[16], (N_ATT_LAYERS, MIX_WIDTH, D_MODEL), f32) * MIX_WIDTH ** -0.5,
        'norm_ffn_g': 1.0 + 0.05 * nrm(ks[17], (DEPTH, D_MODEL), f32),
        'w_ffn_up': nrm(ks[18], (DEPTH, D_MODEL, 2 * D_FF), f32) * D_MODEL ** -0.5,
        'ffn_conv_w': nrm(ks[19], (DEPTH, CONV_W, D_FF), f32) * CONV_W ** -0.5,
        'ffn_conv_b': 0.02 * nrm(ks[20], (DEPTH, D_FF), f32),
        'w_ffn_down': nrm(ks[21], (DEPTH, D_FF, D_MODEL), f32) * D_FF ** -0.5,
        'norm_final_g': 1.0 + 0.05 * nrm(ks[22], (D_MODEL,), f32),
    }


def reference(x_prompt, x_sample, state_ret, state_gla, cache_attn_k, cache_attn_v, state_ffn_conv,
              norm_mix_g, w_in_ab, gla_gate_w2, gla_gate_b, ret_norm_g, gla_norm_g, w_out_ab,
              w_qkv_att, rel_bias_att, w_out_att,
              norm_ffn_g, w_ffn_up, ffn_conv_w, ffn_conv_b, w_ffn_down, norm_final_g):
    p = dict(norm_mix_g=norm_mix_g, w_in_ab=w_in_ab, gla_gate_w2=gla_gate_w2, gla_gate_b=gla_gate_b,
             ret_norm_g=ret_norm_g, gla_norm_g=gla_norm_g, w_out_ab=w_out_ab,
             w_qkv_att=w_qkv_att, rel_bias_att=rel_bias_att, w_out_att=w_out_att,
             norm_ffn_g=norm_ffn_g, w_ffn_up=w_ffn_up, ffn_conv_w=ffn_conv_w, ffn_conv_b=ffn_conv_b,
             w_ffn_down=w_ffn_down, norm_final_g=norm_final_g)
    pos_p = jnp.arange(x_prompt.shape[1])
    pos_s = PAST_LEN + jnp.arange(x_sample.shape[1])
    y_prompt, p_ret, p_gla, p_k, p_v, p_conv = trunk(x_prompt, pos_p, None, None, None, None, None, p)
    y_sample, s_ret, s_gla, s_k, s_v, s_conv = trunk(x_sample, pos_s, state_ret, state_gla,
                                                     cache_attn_k, cache_attn_v, state_ffn_conv, p)
    return (y_prompt, y_sample, p_ret, p_gla, p_k, p_v, p_conv, s_ret, s_gla, s_k, s_v, s_conv)
```

```python
import functools
import math

import jax
import jax.numpy as jnp
import numpy as np
from jax import lax
from jax.experimental import pallas as pl
from jax.experimental.pallas import tpu as pltpu

F32 = jnp.float32
BF16 = jnp.bfloat16

CHUNK = 64
RMS_EPS = 1e-6
H_RET = 4
H_GLA = 4
GLA_GATE_RANK = 16
GLA_GATE_TAU = 16.0
ROPE_BASE = 10000.0
H_ATT = 16
LEFT_CHUNKS = 8
MAX_REL = 256
NEG_INF = -1e30
CONV_W = 3
PAST_LEN = 2048

LANES = 128
SUBLANES = 8
VMEM_LIMIT = 52 * 1024 * 1024

BAND = (LEFT_CHUNKS + 1) * CHUNK
Q_TILE = 2 * CHUNK
K_WIN = BAND + CHUNK
TOEP_W = 1024


def _cparams(sem):
    return pltpu.CompilerParams(dimension_semantics=sem, vmem_limit_bytes=VMEM_LIMIT)


def _rms(x, g):
    return x * lax.rsqrt(jnp.mean(x * x, axis=-1, keepdims=True) + RMS_EPS) * g


def _silu(x):
    return x * jax.nn.sigmoid(x)


def _dot(a, b):
    return jnp.dot(a, b, preferred_element_type=F32)


def _dot_nt(a, b):
    return lax.dot_general(a, b, (((1,), (1,)), ((), ())), preferred_element_type=F32)


def _dot_tn(a, b):
    return lax.dot_general(a, b, (((0,), (0,)), ((), ())), preferred_element_type=F32)


def _norm_mm_kernel(x_ref, g_ref, w_ref, o_ref, h_sc):
    @pl.when(pl.program_id(1) == 0)
    def _():
        h_sc[...] = _rms(x_ref[...], g_ref[...]).astype(BF16)

    o_ref[...] = _dot(h_sc[...], w_ref[...])


def _norm_mm_gate_kernel(x_ref, g_ref, w_ref, wlo_ref, w2_ref, gb_ref, o_ref, la_ref, h_sc):
    @pl.when(pl.program_id(1) == 0)
    def _():
        h = _rms(x_ref[...], g_ref[...]).astype(BF16)
        h_sc[...] = h
        lo = _dot(h, wlo_ref[...])
        xg = _dot(lo.astype(BF16), w2_ref[...]) + gb_ref[...]
        la_ref[...] = jax.nn.log_sigmoid(xg) / GLA_GATE_TAU

    o_ref[...] = _dot(h_sc[...], w_ref[...])


def _norm_matmul(x, g, w, *, bm, bn, gate=None):
    M, D = x.shape
    N = w.shape[1]
    bm = min(bm, M)
    grid = (M // bm, N // bn)
    x_spec = pl.BlockSpec((bm, D), lambda i, j: (i, 0))
    g_spec = pl.BlockSpec((1, D), lambda i, j: (0, 0))
    w_spec = pl.BlockSpec((D, bn), lambda i, j: (0, j))
    o_spec = pl.BlockSpec((bm, bn), lambda i, j: (i, j))
    scratch = [pltpu.VMEM((bm, D), BF16)]
    if gate is None:
        return pl.pallas_call(
            _norm_mm_kernel, grid=grid, in_specs=[x_spec, g_spec, w_spec], out_specs=o_spec,
            out_shape=jax.ShapeDtypeStruct((M, N), F32), scratch_shapes=scratch,
            compiler_params=_cparams(("parallel", "arbitrary")), name="norm_matmul")(x, g, w)
    wlo, w2, gb = gate
    G = w2.shape[1]
    return pl.pallas_call(
        _norm_mm_gate_kernel, grid=grid,
        in_specs=[x_spec, g_spec, w_spec,
                  pl.BlockSpec(wlo.shape, lambda i, j: (0, 0)),
                  pl.BlockSpec(w2.shape, lambda i, j: (0, 0)),
                  pl.BlockSpec((1, G), lambda i, j: (0, 0))],
        out_specs=[o_spec, pl.BlockSpec((bm, G), lambda i, j: (i, 0))],
        out_shape=[jax.ShapeDtypeStruct((M, N), F32), jax.ShapeDtypeStruct((M, G), F32)],
        scratch_shapes=scratch,
        compiler_params=_cparams(("parallel", "arbitrary")), name="norm_matmul_gate")(x, g, w, wlo, w2, gb)


def _proj_res_kernel(x_ref, o_ref, w_ref, y_ref):
    y_ref[...] = x_ref[...] + _dot(o_ref[...], w_ref[...])


def _proj_residual(x, o, w, *, bm):
    M, D = x.shape
    K = o.shape[1]
    bm = min(bm, M)
    return pl.pallas_call(
        _proj_res_kernel, grid=(M // bm,),
        in_specs=[pl.BlockSpec((bm, D), lambda i: (i, 0)),
                  pl.BlockSpec((bm, K), lambda i: (i, 0)),
                  pl.BlockSpec((K, D), lambda i: (0, 0))],
        out_specs=pl.BlockSpec((bm, D), lambda i: (i, 0)),
        out_shape=jax.ShapeDtypeStruct((M, D), F32),
        compiler_params=_cparams(("parallel",)), name="proj_residual")(x, o, w)


def _ffn_kernel(*refs, nseq, seq_rows, tiles_per_seq, final_norm):
    refs = list(refs)
    x_ref, g_ref, wg_ref, wu_ref, cw_ref, cb_ref, wd_ref = refs[:7]
    pos = 7
    hist_ref = None
    if tiles_per_seq is None:
        hist_ref = refs[pos]
        pos += 1
    gf_ref = None
    if final_norm:
        gf_ref = refs[pos]
        pos += 1
    y_ref, tail_ref = refs[pos:pos + 2]
    h_sc, acc_sc, gbuf_sc = refs[pos + 2:pos + 5]
    carry_sc = refs[pos + 5] if tiles_per_seq is not None else None

    m = pl.program_id(0)
    f = pl.program_id(1)
    nf = pl.num_programs(1)
    L = seq_rows
    H = SUBLANES

    @pl.when(f == 0)
    def _():
        x = x_ref[...]
        h_sc[...] = _rms(x, g_ref[...]).astype(BF16)
        acc_sc[...] = x

    h = h_sc[...]
    gate = _dot(h, wg_ref[...])
    up = _dot(h, wu_ref[...])
    bm, bf = gate.shape
    g3 = gate.reshape(nseq, L, bf)
    gbuf_sc[:, H:H + L, :] = g3
    if tiles_per_seq is None:
        gbuf_sc[:, 0:H, :] = hist_ref[...]
    else:
        first = (m % tiles_per_seq) == 0

        @pl.when(first)
        def _():
            gbuf_sc[:, 0:H, :] = jnp.zeros((nseq, H, bf), F32)

        @pl.when(jnp.logical_not(first))
        def _():
            gbuf_sc[:, 0:H, :] = carry_sc[f]

        carry_sc[f] = g3[:, L - H:L, :]
    tail_ref[...] = g3[:, L - H:L, :]

    g1 = gbuf_sc[:, H - 1:H - 1 + L, :]
    g2 = gbuf_sc[:, H - 2:H - 2 + L, :]
    cw = cw_ref[...]
    gc = cb_ref[...] + g2 * cw[0:1, :]
    gc = gc + g1 * cw[1:2, :]
    gc = gc + g3 * cw[2:3, :]
    act = (_silu(gc) * up.reshape(nseq, L, bf)).reshape(bm, bf).astype(BF16)
    acc_sc[...] += _dot(act, wd_ref[...])

    @pl.when(f == nf - 1)
    def _():
        r = acc_sc[...]
        if final_norm:
            r = _rms(r, gf_ref[...])
        y_ref[...] = r


def _conv_ffn(x, g, w_up, cw, cb, wd, *, bm, bf, seq_len, hist=None, final_g=None):
    M, D = x.shape
    F = wd.shape[0]
    bm = min(bm, M)
    nf = F // bf
    if seq_len >= bm:
        nseq, seq_rows, tiles_per_seq = 1, bm, seq_len // bm
        n_groups = M // bm
    else:
        nseq, seq_rows, tiles_per_seq = bm // seq_len, seq_len, None
        n_groups = M // seq_len
    in_specs = [pl.BlockSpec((bm, D), lambda i, j: (i, 0)),
                pl.BlockSpec((1, D), lambda i, j: (0, 0)),
                pl.BlockSpec((D, bf), lambda i, j: (0, j)),
                pl.BlockSpec((D, bf), lambda i, j: (0, nf + j)),
                pl.BlockSpec((CONV_W, bf), lambda i, j: (0, j)),
                pl.BlockSpec((1, bf), lambda i, j: (0, j)),
                pl.BlockSpec((bf, D), lambda i, j: (j, 0))]
    args = [x, g, w_up, w_up, cw, cb, wd]
    if tiles_per_seq is None:
        in_specs.append(pl.BlockSpec((nseq, SUBLANES, bf), lambda i, j: (i, 0, j)))
        args.append(hist)
    if final_g is not None:
        in_specs.append(pl.BlockSpec((1, D), lambda i, j: (0, 0)))
        args.append(final_g)
    scratch = [pltpu.VMEM((bm, D), BF16), pltpu.VMEM((bm, D), F32),
               pltpu.VMEM((nseq, SUBLANES + seq_rows, bf), F32)]
    if tiles_per_seq is not None:
        scratch.append(pltpu.VMEM((nf, 1, SUBLANES, bf), F32))
    kern = functools.partial(_ffn_kernel, nseq=nseq, seq_rows=seq_rows, tiles_per_seq=tiles_per_seq,
                             final_norm=final_g is not None)
    return pl.pallas_call(
        kern, grid=(M // bm, nf), in_specs=in_specs,
        out_specs=[pl.BlockSpec((bm, D), lambda i, j: (i, 0)),
                   pl.BlockSpec((nseq, SUBLANES, bf), lambda i, j: (i, 0, j))],
        out_shape=[jax.ShapeDtypeStruct((M, D), F32), jax.ShapeDtypeStruct((n_groups, SUBLANES, F), F32)],
        scratch_shapes=scratch,
        compiler_params=_cparams(("arbitrary", "arbitrary")), name="conv_ffn")(*args)


def _ab_kernel(*refs, C, n_sub, has_state):
    refs = list(refs)
    z_ref, la_ref, cos_ref, sin_ref, rg_ref, gg_ref = refs[:6]
    pos = 6
    if has_state:
        sr0_ref, sg0_ref = refs[pos:pos + 2]
        pos += 2
    o_ref, sr_out_ref, sg_out_ref = refs[pos:pos + 3]
    sret_sc, sgla_sc = refs[pos + 3:pos + 5]

    t = pl.program_id(1)
    nt = pl.num_programs(1)
    DK_R = 256
    DV = 256
    DK_G = 128

    @pl.when(t == 0)
    def _():
        if has_state:
            sret_sc[...] = sr0_ref[0]
            sgla_sc[...] = sg0_ref[0]
        else:
            sret_sc[...] = jnp.zeros(sret_sc.shape, F32)
            sgla_sc[...] = jnp.zeros(sgla_sc.shape, F32)

    row = lax.broadcasted_iota(jnp.int32, (C, C), 0)
    col = lax.broadcasted_iota(jnp.int32, (C, C), 1)
    causal = row >= col
    diff = jnp.where(causal, row - col, 0).astype(F32)
    tri = jnp.where(causal, 1.0, 0.0).astype(BF16)
    ridx = lax.broadcasted_iota(jnp.int32, (C, 1), 0).astype(F32)
    lane = lax.broadcasted_iota(jnp.int32, (SUBLANES, LANES), 1)
    sub = lax.broadcasted_iota(jnp.int32, (SUBLANES, LANES), 0)
    ones = jnp.ones((LANES, LANES), BF16)
    rg = rg_ref[...]
    gg = gg_ref[...]

    def chunk(c, carry):
        r0 = pl.multiple_of(c * C, C)
        rows = pl.ds(r0, C)
        cos = cos_ref[rows, :]
        sin = sin_ref[rows, :]

        def rope(x):
            x1 = x[:, :LANES]
            x2 = x[:, LANES:]
            return jnp.concatenate([x1 * cos - x2 * sin, x1 * sin + x2 * cos], axis=-1)

        for h in range(H_RET):
            lg = math.log1p(-2.0 ** (-5.0 - h))
            q = rope(z_ref[rows, pl.ds(h * DK_R, DK_R)])
            k = rope(z_ref[rows, pl.ds(H_RET * DK_R + h * DK_R, DK_R)]) * (DK_R ** -0.5)
            v = z_ref[rows, pl.ds(2 * H_RET * DK_R + h * DV, DV)].astype(BF16)
            gate = z_ref[rows, pl.ds(2 * H_RET * DK_R + H_RET * DV + h * DV, DV)]
            S = sret_sc[h]
            decay = jnp.where(causal, jnp.exp(diff * lg), 0.0)
            scores = _dot_nt(q.astype(BF16), k.astype(BF16)) * decay
            o = _dot(scores.astype(BF16), v)
            q_dec = jnp.exp((ridx + 1.0) * lg)
            o = o + _dot((q * q_dec).astype(BF16), S.astype(BF16))
            k_dec = jnp.exp((C - 1.0 - ridx) * lg)
            sret_sc[h] = math.exp(C * lg) * S + _dot_tn((k * k_dec).astype(BF16), v)
            o = _rms(o, rg) * _silu(gate)
            o_ref[rows, pl.ds(h * DV, DV)] = o.astype(o_ref.dtype)

        base = 2 * H_RET * DK_R + 2 * H_RET * DV
        for h in range(H_GLA):
            q = z_ref[rows, pl.ds(base + h * DK_G, DK_G)] * (DK_G ** -0.5)
            k = z_ref[rows, pl.ds(base + H_GLA * DK_G + h * DK_G, DK_G)]
            v = z_ref[rows, pl.ds(base + 2 * H_GLA * DK_G + h * DV, DV)].astype(BF16)
            gate = z_ref[rows, pl.ds(base + 2 * H_GLA * DK_G + H_GLA * DV + h * DV, DV)]
            la = la_ref[rows, pl.ds(h * DK_G, DK_G)]
            la_hi = la.astype(BF16)
            la_lo = (la - la_hi.astype(F32)).astype(BF16)
            b2 = _dot(tri, jnp.concatenate([la_hi, la_lo], axis=-1))
            b = b2[:, :DK_G] + b2[:, DK_G:]
            S = sgla_sc[h]
            o = _dot((q * jnp.exp(b)).astype(BF16), S.astype(BF16))
            b_last = b[C - 1:C, :]
            kd = k * jnp.exp(b_last - b)
            e_col = jnp.transpose(jnp.broadcast_to(jnp.exp(b_last), (LANES, DK_G)))[:, 0:1]
            sgla_sc[h] = e_col * S + _dot_tn(kd.astype(BF16), v)
            pieces = []
            offs = []
            total = 0
            for j in range(C):
                rb0 = (j // SUBLANES) * SUBLANES
                bj = b[j:j + 1, :]
                kj = k[j:j + 1, :]
                bi = b[rb0:, :]
                valid = (lax.broadcasted_iota(jnp.int32, bi.shape, 0) + rb0) >= j
                w = jnp.exp(jnp.where(valid, bi - bj, 0.0))
                pieces.append(jnp.where(valid, q[rb0:, :] * kj * w, 0.0))
                offs.append(total)
                total += C - rb0
            sums = _dot(jnp.concatenate(pieces, axis=0).astype(BF16), ones)
            blocks = []
            for rb in range(C // SUBLANES):
                acc = jnp.zeros((SUBLANES, LANES), F32)
                for j in range(rb * SUBLANES + SUBLANES):
                    rb0 = (j // SUBLANES) * SUBLANES
                    start = offs[j] + rb * SUBLANES - rb0
                    acc = jnp.where(lane == j, sums[start:start + SUBLANES, :], acc)
                blocks.append(acc)
            scores = jnp.concatenate(blocks, axis=0)[:, :C]
            o = o + _dot(scores.astype(BF16), v)
            o = _rms(o, gg) * _silu(gate)
            o_ref[rows, pl.ds(H_RET * DV + h * DV, DV)] = o.astype(o_ref.dtype)
        return carry

    lax.fori_loop(0, n_sub, chunk, 0)

    @pl.when(t == nt - 1)
    def _():
        sr_out_ref[0] = sret_sc[...]
        sg_out_ref[0] = sgla_sc[...]


def _ab_mixer(z, la, cos, sin, rg, gg, *, n_seq, seq_len, chunk, rows_per_step, s_ret=None, s_gla=None):
    M, ZW = z.shape
    R = min(rows_per_step, seq_len)
    nt = seq_len // R
    has_state = s_ret is not None
    in_specs = [pl.BlockSpec((R, ZW), lambda b, t: (b * nt + t, 0)),
                pl.BlockSpec((R, la.shape[1]), lambda b, t: (b * nt + t, 0)),
                pl.BlockSpec((R, LANES), lambda b, t: (t, 0)),
                pl.BlockSpec((R, LANES), lambda b, t: (t, 0)),
                pl.BlockSpec((1, 256), lambda b, t: (0, 0)),
                pl.BlockSpec((1, 256), lambda b, t: (0, 0))]
    args = [z, la, cos, sin, rg, gg]
    sr_spec = pl.BlockSpec((1, H_RET, 256, 256), lambda b, t: (b, 0, 0, 0))
    sg_spec = pl.BlockSpec((1, H_GLA, 128, 256), lambda b, t: (b, 0, 0, 0))
    if has_state:
        in_specs += [sr_spec, sg_spec]
        args += [s_ret, s_gla]
    kern = functools.partial(_ab_kernel, C=chunk, n_sub=R // chunk, has_state=has_state)
    return pl.pallas_call(
        kern, grid=(n_seq, nt), in_specs=in_specs,
        out_specs=[pl.BlockSpec((R, 2048), lambda b, t: (b * nt + t, 0)), sr_spec, sg_spec],
        out_shape=[jax.ShapeDtypeStruct((M, 2048), BF16),
                   jax.ShapeDtypeStruct((n_seq, H_RET, 256, 256), F32),
                   jax.ShapeDtypeStruct((n_seq, H_GLA, 128, 256), F32)],
        scratch_shapes=[pltpu.VMEM((H_RET, 256, 256), F32), pltpu.VMEM((H_GLA, 128, 256), F32)],
        compiler_params=_cparams(("parallel", "arbitrary")), name="ab_mixer")(*args)


def _bias_kernel(rb_ref, o_ref):
    NB = rb_ref.shape[1]
    u = lax.broadcasted_iota(jnp.int32, (NB, TOEP_W), 1)
    d = jnp.where(u < TOEP_W - Q_TILE, u, u - TOEP_W)
    idx = jnp.clip(LEFT_CHUNKS * CHUNK - d, -MAX_REL, MAX_REL) + MAX_REL
    mrow = lax.broadcasted_iota(jnp.int32, (NB, TOEP_W), 0)
    onehot = jnp.where(mrow == idx, 1.0, 0.0).astype(BF16)
    rb = rb_ref[...]
    hi = rb.astype(BF16)
    r1 = rb - hi.astype(F32)
    mid = r1.astype(BF16)
    lo = (r1 - mid.astype(F32)).astype(BF16)
    grow = _dot(hi, onehot) + _dot(mid, onehot) + _dot(lo, onehot)
    i = lax.broadcasted_iota(jnp.int32, (Q_TILE, K_WIN), 0)
    j = lax.broadcasted_iota(jnp.int32, (Q_TILE, K_WIN), 1)
    rel_chunk = j // CHUNK - i // CHUNK
    in_band = (rel_chunk >= 0) & (rel_chunk <= LEFT_CHUNKS)
    for h in range(rb_ref.shape[0]):
        g = jnp.broadcast_to(grow[h:h + 1, :], (Q_TILE, TOEP_W))
        toep = pltpu.roll(g, 0, 1, stride=1, stride_axis=0)[:, :K_WIN]
        o_ref[h] = jnp.where(in_band, toep, NEG_INF)


def _bias_tiles(rel_bias):
    H, NR = rel_bias.shape
    NB = 5 * LANES
    rbp = jnp.pad(rel_bias, ((0, 0), (0, NB - NR)))
    return pl.pallas_call(
        _bias_kernel, out_shape=jax.ShapeDtypeStruct((H, Q_TILE, K_WIN), F32),
        compiler_params=pltpu.CompilerParams(vmem_limit_bytes=VMEM_LIMIT), name="rel_bias_tiles")(rbp)


def _attn_prompt_kernel(q_ref, k_ref, v_ref, bias_ref, o_ref, kpad_sc, vpad_sc):
    T, DH = k_ref.shape
    PAD = LEFT_CHUNKS * CHUNK
    kpad_sc[0:PAD, :] = jnp.zeros((PAD, DH), BF16)
    vpad_sc[0:PAD, :] = jnp.zeros((PAD, DH), BF16)
    kpad_sc[PAD:PAD + T, :] = k_ref[...].astype(BF16)
    vpad_sc[PAD:PAD + T, :] = v_ref[...].astype(BF16)
    scale = DH ** -0.5
    colj = lax.broadcasted_iota(jnp.int32, (Q_TILE, K_WIN), 1)

    def qtile(qt, carry):
        r0 = pl.multiple_of(qt * Q_TILE, Q_TILE)
        q = q_ref[pl.ds(r0, Q_TILE), :].astype(BF16)
        kw = kpad_sc[pl.ds(r0, K_WIN), :]
        vw = vpad_sc[pl.ds(r0, K_WIN), :]
        s = _dot_nt(q, kw) * scale + bias_ref[0]
        s = jnp.where(colj + r0 >= PAD, s, NEG_INF)
        mx = jnp.max(s, axis=-1, keepdims=True)
        p = jnp.exp(s - mx)
        den = jnp.sum(p, axis=-1, keepdims=True)
        o = _dot(p.astype(BF16), vw) / den
        o_ref[pl.ds(r0, Q_TILE), :] = o.astype(o_ref.dtype)
        return carry

    lax.fori_loop(0, T // Q_TILE, qtile, 0)


def _attn_prompt(z, bias, *, n_seq, seq_len):
    M = z.shape[0]
    DH = 128
    T = seq_len
    return pl.pallas_call(
        _attn_prompt_kernel, grid=(n_seq, H_ATT),
        in_specs=[pl.BlockSpec((T, DH), lambda b, h: (b, h)),
                  pl.BlockSpec((T, DH), lambda b, h: (b, H_ATT + h)),
                  pl.BlockSpec((T, DH), lambda b, h: (b, 2 * H_ATT + h)),
                  pl.BlockSpec((1, Q_TILE, K_WIN), lambda b, h: (h, 0, 0))],
        out_specs=pl.BlockSpec((T, DH), lambda b, h: (b, h)),
        out_shape=jax.ShapeDtypeStruct((M, H_ATT * DH), BF16),
        scratch_shapes=[pltpu.VMEM((LEFT_CHUNKS * CHUNK + T, DH), BF16),
                        pltpu.VMEM((LEFT_CHUNKS * CHUNK + T, DH), BF16)],
        compiler_params=_cparams(("parallel", "parallel")), name="attn_prompt")(z, z, z, bias)


def _attn_sample_kernel(q_ref, k_ref, v_ref, kc_ref, vc_ref, bias_ref, o_ref):
    T = q_ref.shape[0]
    W = kc_ref.shape[1]
    DH = 128
    scale = DH ** -0.5
    for h in range(H_ATT):
        cs = pl.ds(h * DH, DH)
        q = q_ref[:, cs].astype(BF16)
        kn = k_ref[:, cs].astype(BF16)
        vn = v_ref[:, cs].astype(BF16)
        kc = kc_ref[0, :, cs].astype(BF16)
        vc = vc_ref[0, :, cs].astype(BF16)
        s1 = _dot_nt(q, kc) * scale + bias_ref[h, :, 0:W]
        s2 = _dot_nt(q, kn) * scale + bias_ref[h, :, W:W + T]
        mx = jnp.maximum(jnp.max(s1, axis=-1, keepdims=True), jnp.max(s2, axis=-1, keepdims=True))
        p1 = jnp.exp(s1 - mx)
        p2 = jnp.exp(s2 - mx)
        den = jnp.sum(p1, axis=-1, keepdims=True) + jnp.sum(p2, axis=-1, keepdims=True)
        o = (_dot(p1.astype(BF16), vc) + _dot(p2.astype(BF16), vn)) / den
        o_ref[:, cs] = o.astype(o_ref.dtype)


def _attn_sample(z, kc, vc, bias, *, n_seq, seq_len):
    M = z.shape[0]
    T = seq_len
    E = H_ATT * 128
    W = kc.shape[1]
    return pl.pallas_call(
        _attn_sample_kernel, grid=(n_seq,),
        in_specs=[pl.BlockSpec((T, E), lambda b: (b, 0)),
                  pl.BlockSpec((T, E), lambda b: (b, 1)),
                  pl.BlockSpec((T, E), lambda b: (b, 2)),
                  pl.BlockSpec((1, W, E), lambda b: (b, 0, 0)),
                  pl.BlockSpec((1, W, E), lambda b: (b, 0, 0)),
                  pl.BlockSpec((H_ATT, T, K_WIN), lambda b: (0, 0, 0))],
        out_specs=pl.BlockSpec((T, E), lambda b: (b, 0)),
        out_shape=jax.ShapeDtypeStruct((M, E), BF16),
        compiler_params=_cparams(("parallel",)), name="attn_sample")(z, z, z, kc, vc, bias)


def _rope_tables(pos):
    half = LANES
    inv = ROPE_BASE ** (-jnp.arange(half, dtype=F32) / half)
    ang = pos.astype(F32)[:, None] * inv[None, :]
    return jnp.cos(ang), jnp.sin(ang)


def _trunk(x, pos, wts, bias, *, n_seq, seq_len, state=None):
    D = x.shape[-1]
    M = n_seq * seq_len
    x = x.reshape(M, D)
    sample = state is not None
    chunk = min(CHUNK, seq_len)
    cos, sin = _rope_tables(pos)

    z, la = _norm_matmul(x, wts['norm_mix_g'][0], wts['w_in_main'], bm=512, bn=1024,
                         gate=(wts['w_in_lo'], wts['gate_w2'], wts['gate_b']))
    if sample:
        o, s_ret, s_gla = _ab_mixer(z, la, cos, sin, wts['ret_g'], wts['gla_g'], n_seq=n_seq, seq_len=seq_len,
                                    chunk=chunk, rows_per_step=256, s_ret=state['ret'], s_gla=state['gla'])
    else:
        o, s_ret, s_gla = _ab_mixer(z, la, cos, sin, wts['ret_g'], wts['gla_g'], n_seq=n_seq, seq_len=seq_len,
                                    chunk=chunk, rows_per_step=256)
    x = _proj_residual(x, o, wts['w_out_ab'], bm=512)
    hist0 = state['conv'][0] if sample else None
    x, tail0 = _conv_ffn(x, wts['norm_ffn_g'][0], wts['w_up'][0], wts['conv_w'][0], wts['conv_b'][0],
                         wts['w_down'][0], bm=512, bf=512, seq_len=seq_len, hist=hist0)

    z = _norm_matmul(x, wts['norm_mix_g'][1], wts['w_qkv'], bm=512, bn=1024)
    E = H_ATT * 128
    k_all = z[:, E:2 * E].reshape(n_seq, seq_len, H_ATT, 128)
    v_all = z[:, 2 * E:].reshape(n_seq, seq_len, H_ATT, 128)
    if sample:
        o = _attn_sample(z, state['k'], state['v'], bias, n_seq=n_seq, seq_len=seq_len)
        new_k, new_v = k_all, v_all
    else:
        o = _attn_prompt(z, bias, n_seq=n_seq, seq_len=seq_len)
        keep = min(LEFT_CHUNKS * CHUNK, seq_len)
        new_k, new_v = k_all[:, seq_len - keep:], v_all[:, seq_len - keep:]
    x = _proj_residual(x, o, wts['w_out_att'], bm=512)
    hist1 = state['conv'][1] if sample else None
    y, tail1 = _conv_ffn(x, wts['norm_ffn_g'][1], wts['w_up'][1], wts['conv_w'][1], wts['conv_b'][1],
                         wts['w_down'][1], bm=512, bf=512, seq_len=seq_len, hist=hist1,
                         final_g=wts['norm_final_g'])

    def conv_state(tail):
        F = tail.shape[-1]
        t = tail.reshape(n_seq, -1, SUBLANES, F)[:, -1]
        return t[:, SUBLANES - (CONV_W - 1):]

    conv = jnp.stack([conv_state(tail0), conv_state(tail1)])
    return (y.reshape(n_seq, seq_len, D), s_ret[None], s_gla[None], new_k[None], new_v[None], conv)


def kernel(x_prompt, x_sample, state_ret, state_gla, cache_attn_k, cache_attn_v, state_ffn_conv, norm_mix_g, w_in_ab, gla_gate_w2, gla_gate_b, ret_norm_g, gla_norm_g, w_out_ab, w_qkv_att, rel_bias_att, w_out_att, norm_ffn_g, w_ffn_up, ffn_conv_w, ffn_conv_b, w_ffn_down, norm_final_g):
    B, T, D = x_prompt.shape
    SB, ST, _ = x_sample.shape
    main = w_in_ab.shape[2] - GLA_GATE_RANK
    wts = dict(
        norm_mix_g=norm_mix_g[:, None, :],
        w_in_main=w_in_ab[0, :, :main].astype(BF16),
        w_in_lo=jnp.pad(w_in_ab[0, :, main:], ((0, 0), (0, LANES - GLA_GATE_RANK))).astype(BF16),
        gate_w2=jnp.pad(gla_gate_w2[0], ((0, LANES - GLA_GATE_RANK), (0, 0))).astype(BF16),
        gate_b=gla_gate_b[0][None, :],
        ret_g=ret_norm_g[0][None, :],
        gla_g=gla_norm_g[0][None, :],
        w_out_ab=w_out_ab[0].astype(BF16),
        w_qkv=w_qkv_att[0].astype(BF16),
        w_out_att=w_out_att[0].astype(BF16),
        norm_ffn_g=norm_ffn_g[:, None, :],
        w_up=w_ffn_up.astype(BF16),
        conv_w=ffn_conv_w,
        conv_b=ffn_conv_b[:, None, :],
        w_down=w_ffn_down.astype(BF16),
        norm_final_g=norm_final_g[None, :],
    )
    bias = _bias_tiles(rel_bias_att[0])
    W = cache_attn_k.shape[2]
    state = dict(
        ret=state_ret[0], gla=state_gla[0],
        k=cache_attn_k[0].reshape(SB, W, -1), v=cache_attn_v[0].reshape(SB, W, -1),
        conv=jnp.pad(state_ffn_conv, ((0, 0), (0, 0), (SUBLANES - (CONV_W - 1), 0), (0, 0))),
    )
    outs_p = _trunk(x_prompt, jnp.arange(T), wts, bias, n_seq=B, seq_len=T)
    outs_s = _trunk(x_sample, PAST_LEN + jnp.arange(ST), wts, bias, n_seq=SB, seq_len=ST, state=state)
    return (outs_p[0], outs_s[0]) + outs_p[1:] + outs_s[1:]
```

```python
import functools
import math

import jax
import jax.numpy as jnp
import numpy as np
from jax import lax
from jax.experimental import pallas as pl
from jax.experimental.pallas import tpu as pltpu

F32 = jnp.float32
BF16 = jnp.bfloat16

CHUNK = 64
RMS_EPS = 1e-6
H_RET = 4
H_GLA = 4
GLA_GATE_RANK = 16
GLA_GATE_TAU = 16.0
ROPE_BASE = 10000.0
H_ATT = 16
LEFT_CHUNKS = 8
MAX_REL = 256
NEG_INF = -1e30
CONV_W = 3
PAST_LEN = 2048

LANES = 128
SUBLANES = 8
VMEM_LIMIT = 60 * 1024 * 1024

BAND = (LEFT_CHUNKS + 1) * CHUNK
Q_TILE = 2 * CHUNK
K_WIN = BAND + CHUNK
TOEP_W = 1024

MM_BM, MM_BN = 1024, 1024
PROJ_BM = 512
FFN_BM, FFN_BF = 1024, 512


def _cparams(sem):
    return pltpu.CompilerParams(dimension_semantics=sem, vmem_limit_bytes=VMEM_LIMIT)


def _rms(x, g):
    return x * lax.rsqrt(jnp.mean(x * x, axis=-1, keepdims=True) + RMS_EPS) * g


def _silu(x):
    return x * jax.nn.sigmoid(x)


def _dot(a, b):
    return jnp.dot(a, b, preferred_element_type=F32)


def _dot_nt(a, b):
    return lax.dot_general(a, b, (((1,), (1,)), ((), ())), preferred_element_type=F32)


def _dot_tn(a, b):
    return lax.dot_general(a, b, (((0,), (0,)), ((), ())), preferred_element_type=F32)


def _norm_mm_kernel(x_ref, g_ref, w_ref, o_ref, h_sc):
    @pl.when(pl.program_id(1) == 0)
    def _():
        h_sc[...] = _rms(x_ref[...], g_ref[...]).astype(BF16)

    o_ref[...] = _dot(h_sc[...], w_ref[...])


def _norm_mm_gate_kernel(x_ref, g_ref, w_ref, wlo_ref, w2_ref, gb_ref, o_ref, la_ref, h_sc):
    @pl.when(pl.program_id(1) == 0)
    def _():
        h = _rms(x_ref[...], g_ref[...]).astype(BF16)
        h_sc[...] = h
        lo = _dot(h, wlo_ref[...])
        xg = _dot(lo.astype(BF16), w2_ref[...]) + gb_ref[...]
        la_ref[...] = jax.nn.log_sigmoid(xg) / GLA_GATE_TAU

    o_ref[...] = _dot(h_sc[...], w_ref[...])


def _norm_matmul(x, g, w, *, bm, bn, gate=None):
    M, D = x.shape
    N = w.shape[1]
    bm = min(bm, M)
    grid = (M // bm, N // bn)
    x_spec = pl.BlockSpec((bm, D), lambda i, j: (i, 0))
    g_spec = pl.BlockSpec((1, D), lambda i, j: (0, 0))
    w_spec = pl.BlockSpec((D, bn), lambda i, j: (0, j))
    o_spec = pl.BlockSpec((bm, bn), lambda i, j: (i, j))
    scratch = [pltpu.VMEM((bm, D), BF16)]
    if gate is None:
        return pl.pallas_call(
            _norm_mm_kernel, grid=grid, in_specs=[x_spec, g_spec, w_spec], out_specs=o_spec,
            out_shape=jax.ShapeDtypeStruct((M, N), F32), scratch_shapes=scratch,
            compiler_params=_cparams(("parallel", "arbitrary")), name="norm_matmul")(x, g, w)
    wlo, w2, gb = gate
    G = w2.shape[1]
    return pl.pallas_call(
        _norm_mm_gate_kernel, grid=grid,
        in_specs=[x_spec, g_spec, w_spec,
                  pl.BlockSpec(wlo.shape, lambda i, j: (0, 0)),
                  pl.BlockSpec(w2.shape, lambda i, j: (0, 0)),
                  pl.BlockSpec((1, G), lambda i, j: (0, 0))],
        out_specs=[o_spec, pl.BlockSpec((bm, G), lambda i, j: (i, 0))],
        out_shape=[jax.ShapeDtypeStruct((M, N), F32), jax.ShapeDtypeStruct((M, G), F32)],
        scratch_shapes=scratch,
        compiler_params=_cparams(("parallel", "arbitrary")), name="norm_matmul_gate")(x, g, w, wlo, w2, gb)


def _proj_res_kernel(x_ref, o_ref, w_ref, y_ref):
    y_ref[...] = x_ref[...] + _dot(o_ref[...], w_ref[...])


def _proj_residual(x, o, w, *, bm):
    M, D = x.shape
    K = o.shape[1]
    bm = min(bm, M)
    return pl.pallas_call(
        _proj_res_kernel, grid=(M // bm,),
        in_specs=[pl.BlockSpec((bm, D), lambda i: (i, 0)),
                  pl.BlockSpec((bm, K), lambda i: (i, 0)),
                  pl.BlockSpec((K, D), lambda i: (0, 0))],
        out_specs=pl.BlockSpec((bm, D), lambda i: (i, 0)),
        out_shape=jax.ShapeDtypeStruct((M, D), F32),
        compiler_params=_cparams(("parallel",)), name="proj_residual")(x, o, w)


def _ffn_kernel(*refs, nseq, seq_rows, tiles_per_seq, final_norm):
    refs = list(refs)
    x_ref, g_ref, wg_ref, wu_ref, cw_ref, cb_ref, wd_ref = refs[:7]
    pos = 7
    hist_ref = None
    if tiles_per_seq is None:
        hist_ref = refs[pos]
        pos += 1
    gf_ref = None
    if final_norm:
        gf_ref = refs[pos]
        pos += 1
    y_ref, tail_ref = refs[pos:pos + 2]
    h_sc, gbuf_sc = refs[pos + 2:pos + 4]
    carry_sc = refs[pos + 4] if tiles_per_seq is not None else None

    m = pl.program_id(0)
    f = pl.program_id(1)
    nf = pl.num_programs(1)
    L = seq_rows
    H = SUBLANES

    @pl.when(f == 0)
    def _():
        x = x_ref[...]
        h_sc[...] = _rms(x, g_ref[...]).astype(BF16)
        y_ref[...] = x

    h = h_sc[...]
    gate = _dot(h, wg_ref[...])
    up = _dot(h, wu_ref[...])
    bm, bf = gate.shape
    g3 = gate.reshape(nseq, L, bf)
    gbuf_sc[:, H:H + L, :] = g3
    if tiles_per_seq is None:
        gbuf_sc[:, 0:H, :] = hist_ref[...]
    else:
        first = (m % tiles_per_seq) == 0

        @pl.when(first)
        def _():
            gbuf_sc[:, 0:H, :] = jnp.zeros((nseq, H, bf), F32)

        @pl.when(jnp.logical_not(first))
        def _():
            gbuf_sc[:, 0:H, :] = carry_sc[f]

        carry_sc[f] = g3[:, L - H:L, :]
    tail_ref[...] = g3[:, L - H:L, :]

    g1 = gbuf_sc[:, H - 1:H - 1 + L, :]
    g2 = gbuf_sc[:, H - 2:H - 2 + L, :]
    cw = cw_ref[...]
    gc = cb_ref[...] + g2 * cw[0:1, :]
    gc = gc + g1 * cw[1:2, :]
    gc = gc + g3 * cw[2:3, :]
    act = (_silu(gc) * up.reshape(nseq, L, bf)).reshape(bm, bf).astype(BF16)
    y_ref[...] += _dot(act, wd_ref[...])

    if final_norm:
        @pl.when(f == nf - 1)
        def _():
            y_ref[...] = _rms(y_ref[...], gf_ref[...])


def _conv_ffn(x, g, w_up, cw, cb, wd, *, bm, bf, seq_len, hist=None, final_g=None):
    M, D = x.shape
    F = wd.shape[0]
    bm = min(bm, M)
    nf = F // bf
    if seq_len >= bm:
        nseq, seq_rows, tiles_per_seq = 1, bm, seq_len // bm
        n_groups = M // bm
    else:
        nseq, seq_rows, tiles_per_seq = bm // seq_len, seq_len, None
        n_groups = M // seq_len
    in_specs = [pl.BlockSpec((bm, D), lambda i, j: (i, 0)),
                pl.BlockSpec((1, D), lambda i, j: (0, 0)),
                pl.BlockSpec((D, bf), lambda i, j: (0, j)),
                pl.BlockSpec((D, bf), lambda i, j: (0, nf + j)),
                pl.BlockSpec((CONV_W, bf), lambda i, j: (0, j)),
                pl.BlockSpec((1, bf), lambda i, j: (0, j)),
                pl.BlockSpec((bf, D), lambda i, j: (j, 0))]
    args = [x, g, w_up, w_up, cw, cb, wd]
    if tiles_per_seq is None:
        in_specs.append(pl.BlockSpec((nseq, SUBLANES, bf), lambda i, j: (i, 0, j)))
        args.append(hist)
    if final_g is not None:
        in_specs.append(pl.BlockSpec((1, D), lambda i, j: (0, 0)))
        args.append(final_g)
    scratch = [pltpu.VMEM((bm, D), BF16), pltpu.VMEM((nseq, SUBLANES + seq_rows, bf), F32)]
    if tiles_per_seq is not None:
        scratch.append(pltpu.VMEM((nf, 1, SUBLANES, bf), F32))
    kern = functools.partial(_ffn_kernel, nseq=nseq, seq_rows=seq_rows, tiles_per_seq=tiles_per_seq,
                             final_norm=final_g is not None)
    return pl.pallas_call(
        kern, grid=(M // bm, nf), in_specs=in_specs,
        out_specs=[pl.BlockSpec((bm, D), lambda i, j: (i, 0)),
                   pl.BlockSpec((nseq, SUBLANES, bf), lambda i, j: (i, 0, j))],
        out_shape=[jax.ShapeDtypeStruct((M, D), F32), jax.ShapeDtypeStruct((n_groups, SUBLANES, F), F32)],
        scratch_shapes=scratch,
        compiler_params=_cparams(("arbitrary", "arbitrary")), name="conv_ffn")(*args)


def _ab_kernel(*refs, C, n_sub, has_state):
    refs = list(refs)
    z_ref, la_ref, cos_ref, sin_ref, rg_ref, gg_ref = refs[:6]
    pos = 6
    if has_state:
        sr0_ref, sg0_ref = refs[pos:pos + 2]
        pos += 2
    o_ref, sr_out_ref, sg_out_ref = refs[pos:pos + 3]
    sret_sc, sgla_sc = refs[pos + 3:pos + 5]

    t = pl.program_id(1)
    nt = pl.num_programs(1)
    DK_R = 256
    DV = 256
    DK_G = 128

    @pl.when(t == 0)
    def _():
        if has_state:
            sret_sc[...] = sr0_ref[0]
            sgla_sc[...] = sg0_ref[0]
        else:
            sret_sc[...] = jnp.zeros(sret_sc.shape, F32)
            sgla_sc[...] = jnp.zeros(sgla_sc.shape, F32)

    row = lax.broadcasted_iota(jnp.int32, (C, C), 0)
    col = lax.broadcasted_iota(jnp.int32, (C, C), 1)
    causal = row >= col
    diff = jnp.where(causal, row - col, 0).astype(F32)
    tri = jnp.where(causal, 1.0, 0.0).astype(BF16)
    ridx = lax.broadcasted_iota(jnp.int32, (C, 1), 0).astype(F32)
    lane = lax.broadcasted_iota(jnp.int32, (SUBLANES, LANES), 1)
    sub = lax.broadcasted_iota(jnp.int32, (SUBLANES, LANES), 0)
    ones = jnp.ones((LANES, LANES), BF16)
    rg = rg_ref[...]
    gg = gg_ref[...]

    def chunk(c, carry):
        r0 = pl.multiple_of(c * C, C)
        rows = pl.ds(r0, C)
        cos = cos_ref[rows, :]
        sin = sin_ref[rows, :]

        def rope(x):
            x1 = x[:, :LANES]
            x2 = x[:, LANES:]
            return jnp.concatenate([x1 * cos - x2 * sin, x1 * sin + x2 * cos], axis=-1)

        for h in range(H_RET):
            lg = math.log1p(-2.0 ** (-5.0 - h))
            q = rope(z_ref[rows, pl.ds(h * DK_R, DK_R)])
            k = rope(z_ref[rows, pl.ds(H_RET * DK_R + h * DK_R, DK_R)]) * (DK_R ** -0.5)
            v = z_ref[rows, pl.ds(2 * H_RET * DK_R + h * DV, DV)].astype(BF16)
            gate = z_ref[rows, pl.ds(2 * H_RET * DK_R + H_RET * DV + h * DV, DV)]
            S = sret_sc[h]
            decay = jnp.where(causal, jnp.exp(diff * lg), 0.0)
            scores = _dot_nt(q.astype(BF16), k.astype(BF16)) * decay
            o = _dot(scores.astype(BF16), v)
            q_dec = jnp.exp((ridx + 1.0) * lg)
            o = o + _dot((q * q_dec).astype(BF16), S.astype(BF16))
            k_dec = jnp.exp((C - 1.0 - ridx) * lg)
            sret_sc[h] = math.exp(C * lg) * S + _dot_tn((k * k_dec).astype(BF16), v)
            o = _rms(o, rg) * _silu(gate)
            o_ref[rows, pl.ds(h * DV, DV)] = o.astype(o_ref.dtype)

        base = 2 * H_RET * DK_R + 2 * H_RET * DV
        for h in range(H_GLA):
            q = z_ref[rows, pl.ds(base + h * DK_G, DK_G)] * (DK_G ** -0.5)
            k = z_ref[rows, pl.ds(base + H_GLA * DK_G + h * DK_G, DK_G)]
            v = z_ref[rows, pl.ds(base + 2 * H_GLA * DK_G + h * DV, DV)].astype(BF16)
            gate = z_ref[rows, pl.ds(base + 2 * H_GLA * DK_G + H_GLA * DV + h * DV, DV)]
            la = la_ref[rows, pl.ds(h * DK_G, DK_G)]
            la_hi = la.astype(BF16)
            la_lo = (la - la_hi.astype(F32)).astype(BF16)
            b2 = _dot(tri, jnp.concatenate([la_hi, la_lo], axis=-1))
            b = b2[:, :DK_G] + b2[:, DK_G:]
            S = sgla_sc[h]
            o = _dot((q * jnp.exp(b)).astype(BF16), S.astype(BF16))
            b_last = b[C - 1:C, :]
            kd = k * jnp.exp(b_last - b)
            e_col = jnp.transpose(jnp.broadcast_to(jnp.exp(b_last), (LANES, DK_G)))[:, 0:1]
            sgla_sc[h] = e_col * S + _dot_tn(kd.astype(BF16), v)
            pieces = []
            offs = []
            total = 0
            for j in range(C):
                rb0 = (j // SUBLANES) * SUBLANES
                bj = b[j:j + 1, :]
                kj = k[j:j + 1, :]
                bi = b[rb0:, :]
                valid = (lax.broadcasted_iota(jnp.int32, bi.shape, 0) + rb0) >= j
                w = jnp.exp(jnp.where(valid, bi - bj, 0.0))
                pieces.append(jnp.where(valid, q[rb0:, :] * kj * w, 0.0))
                offs.append(total)
                total += C - rb0
            sums = _dot(jnp.concatenate(pieces, axis=0).astype(BF16), ones)
            blocks = []
            for rb in range(C // SUBLANES):
                acc = jnp.zeros((SUBLANES, LANES), F32)
                for j in range(rb * SUBLANES + SUBLANES):
                    rb0 = (j // SUBLANES) * SUBLANES
                    start = offs[j] + rb * SUBLANES - rb0
                    acc = jnp.where(lane == j, sums[start:start + SUBLANES, :], acc)
                blocks.append(acc)
            scores = jnp.concatenate(blocks, axis=0)[:, :C]
            o = o + _dot(scores.astype(BF16), v)
            o = _rms(o, gg) * _silu(gate)
            o_ref[rows, pl.ds(H_RET * DV + h * DV, DV)] = o.astype(o_ref.dtype)
        return carry

    lax.fori_loop(0, n_sub, chunk, 0)

    @pl.when(t == nt - 1)
    def _():
        sr_out_ref[0] = sret_sc[...]
        sg_out_ref[0] = sgla_sc[...]


def _ab_mixer(z, la, cos, sin, rg, gg, *, n_seq, seq_len, chunk, rows_per_step, s_ret=None, s_gla=None):
    M, ZW = z.shape
    R = min(rows_per_step, seq_len)
    nt = seq_len // R
    has_state = s_ret is not None
    in_specs = [pl.BlockSpec((R, ZW), lambda b, t: (b * nt + t, 0)),
                pl.BlockSpec((R, la.shape[1]), lambda b, t: (b * nt + t, 0)),
                pl.BlockSpec((R, LANES), lambda b, t: (t, 0)),
                pl.BlockSpec((R, LANES), lambda b, t: (t, 0)),
                pl.BlockSpec((1, 256), lambda b, t: (0, 0)),
                pl.BlockSpec((1, 256), lambda b, t: (0, 0))]
    args = [z, la, cos, sin, rg, gg]
    sr_spec = pl.BlockSpec((1, H_RET, 256, 256), lambda b, t: (b, 0, 0, 0))
    sg_spec = pl.BlockSpec((1, H_GLA, 128, 256), lambda b, t: (b, 0, 0, 0))
    if has_state:
        in_specs += [sr_spec, sg_spec]
        args += [s_ret, s_gla]
    kern = functools.partial(_ab_kernel, C=chunk, n_sub=R // chunk, has_state=has_state)
    return pl.pallas_call(
        kern, grid=(n_seq, nt), in_specs=in_specs,
        out_specs=[pl.BlockSpec((R, 2048), lambda b, t: (b * nt + t, 0)), sr_spec, sg_spec],
        out_shape=[jax.ShapeDtypeStruct((M, 2048), BF16),
                   jax.ShapeDtypeStruct((n_seq, H_RET, 256, 256), F32),
                   jax.ShapeDtypeStruct((n_seq, H_GLA, 128, 256), F32)],
        scratch_shapes=[pltpu.VMEM((H_RET, 256, 256), F32), pltpu.VMEM((H_GLA, 128, 256), F32)],
        compiler_params=_cparams(("parallel", "arbitrary")), name="ab_mixer")(*args)


def _bias_kernel(rb_ref, o_ref):
    NB = rb_ref.shape[1]
    u = lax.broadcasted_iota(jnp.int32, (NB, TOEP_W), 1)
    d = jnp.where(u < TOEP_W - Q_TILE, u, u - TOEP_W)
    idx = jnp.clip(LEFT_CHUNKS * CHUNK - d, -MAX_REL, MAX_REL) + MAX_REL
    mrow = lax.broadcasted_iota(jnp.int32, (NB, TOEP_W), 0)
    onehot = jnp.where(mrow == idx, 1.0, 0.0).astype(BF16)
    rb = rb_ref[...]
    hi = rb.astype(BF16)
    r1 = rb - hi.astype(F32)
    mid = r1.astype(BF16)
    lo = (r1 - mid.astype(F32)).astype(BF16)
    grow = _dot(hi, onehot) + _dot(mid, onehot) + _dot(lo, onehot)
    i = lax.broadcasted_iota(jnp.int32, (Q_TILE, K_WIN), 0)
    j = lax.broadcasted_iota(jnp.int32, (Q_TILE, K_WIN), 1)
    rel_chunk = j // CHUNK - i // CHUNK
    in_band = (rel_chunk >= 0) & (rel_chunk <= LEFT_CHUNKS)
    for h in range(rb_ref.shape[0]):
        g = jnp.broadcast_to(grow[h:h + 1, :], (Q_TILE, TOEP_W))
        toep = pltpu.roll(g, 0, 1, stride=1, stride_axis=0)[:, :K_WIN]
        o_ref[h] = jnp.where(in_band, toep, NEG_INF)


def _bias_tiles(rel_bias):
    H, NR = rel_bias.shape
    NB = 5 * LANES
    rbp = jnp.pad(rel_bias, ((0, 0), (0, NB - NR)))
    return pl.pallas_call(
        _bias_kernel, out_shape=jax.ShapeDtypeStruct((H, Q_TILE, K_WIN), F32),
        compiler_params=pltpu.CompilerParams(vmem_limit_bytes=VMEM_LIMIT), name="rel_bias_tiles")(rbp)


def _attn_prompt_kernel(q_ref, k_ref, v_ref, bias_ref, o_ref, ko_ref, vo_ref, kpad_sc, vpad_sc):
    T, DH = k_ref.shape
    KEEP = ko_ref.shape[0]
    PAD = LEFT_CHUNKS * CHUNK
    h = pl.program_id(1)
    k = k_ref[...]
    v = v_ref[...]
    ko_ref[:, pl.ds(h, 1), :] = k[T - KEEP:, None, :]
    vo_ref[:, pl.ds(h, 1), :] = v[T - KEEP:, None, :]
    kpad_sc[0:PAD, :] = jnp.zeros((PAD, DH), BF16)
    vpad_sc[0:PAD, :] = jnp.zeros((PAD, DH), BF16)
    kpad_sc[PAD:PAD + T, :] = k.astype(BF16)
    vpad_sc[PAD:PAD + T, :] = v.astype(BF16)
    scale = DH ** -0.5
    colj = lax.broadcasted_iota(jnp.int32, (Q_TILE, K_WIN), 1)

    def qtile(qt, carry):
        r0 = pl.multiple_of(qt * Q_TILE, Q_TILE)
        q = q_ref[pl.ds(r0, Q_TILE), :].astype(BF16)
        kw = kpad_sc[pl.ds(r0, K_WIN), :]
        vw = vpad_sc[pl.ds(r0, K_WIN), :]
        s = _dot_nt(q, kw) * scale + bias_ref[0]
        s = jnp.where(colj + r0 >= PAD, s, NEG_INF)
        mx = jnp.max(s, axis=-1, keepdims=True)
        p = jnp.exp(s - mx)
        den = jnp.sum(p, axis=-1, keepdims=True)
        o = _dot(p.astype(BF16), vw) / den
        o_ref[pl.ds(r0, Q_TILE), :] = o.astype(o_ref.dtype)
        return carry

    lax.fori_loop(0, T // Q_TILE, qtile, 0, unroll=2)


def _attn_prompt(z, bias, *, n_seq, seq_len):
    M = z.shape[0]
    DH = 128
    T = seq_len
    keep = min(LEFT_CHUNKS * CHUNK, T)
    kv_spec = pl.BlockSpec((None, None, keep, H_ATT, DH), lambda b, h: (0, b, 0, 0, 0))
    kv_shape = jax.ShapeDtypeStruct((1, n_seq, keep, H_ATT, DH), F32)
    return pl.pallas_call(
        _attn_prompt_kernel, grid=(n_seq, H_ATT),
        in_specs=[pl.BlockSpec((T, DH), lambda b, h: (b, h)),
                  pl.BlockSpec((T, DH), lambda b, h: (b, H_ATT + h)),
                  pl.BlockSpec((T, DH), lambda b, h: (b, 2 * H_ATT + h)),
                  pl.BlockSpec((1, Q_TILE, K_WIN), lambda b, h: (h, 0, 0))],
        out_specs=[pl.BlockSpec((T, DH), lambda b, h: (b, h)), kv_spec, kv_spec],
        out_shape=[jax.ShapeDtypeStruct((M, H_ATT * DH), BF16), kv_shape, kv_shape],
        scratch_shapes=[pltpu.VMEM((LEFT_CHUNKS * CHUNK + T, DH), BF16),
                        pltpu.VMEM((LEFT_CHUNKS * CHUNK + T, DH), BF16)],
        compiler_params=_cparams(("parallel", "arbitrary")), name="attn_prompt")(z, z, z, bias)


def _attn_sample_kernel(q_ref, k_ref, v_ref, kc_ref, vc_ref, bias_ref, o_ref, ko_ref, vo_ref):
    T = q_ref.shape[0]
    W = kc_ref.shape[0]
    DH = 128
    scale = DH ** -0.5
    for h in range(H_ATT):
        cs = pl.ds(h * DH, DH)
        q = q_ref[:, cs].astype(BF16)
        kn = k_ref[:, cs]
        vn = v_ref[:, cs]
        ko_ref[:, h, :] = kn
        vo_ref[:, h, :] = vn
        kn = kn.astype(BF16)
        vn = vn.astype(BF16)
        kc = kc_ref[:, h, :].astype(BF16)
        vc = vc_ref[:, h, :].astype(BF16)
        s1 = _dot_nt(q, kc) * scale + bias_ref[h, :, 0:W]
        s2 = _dot_nt(q, kn) * scale + bias_ref[h, :, W:W + T]
        mx = jnp.maximum(jnp.max(s1, axis=-1, keepdims=True), jnp.max(s2, axis=-1, keepdims=True))
        p1 = jnp.exp(s1 - mx)
        p2 = jnp.exp(s2 - mx)
        den = jnp.sum(p1, axis=-1, keepdims=True) + jnp.sum(p2, axis=-1, keepdims=True)
        o = (_dot(p1.astype(BF16), vc) + _dot(p2.astype(BF16), vn)) / den
        o_ref[:, cs] = o.astype(o_ref.dtype)


def _attn_sample(z, kc, vc, bias, *, n_seq, seq_len):
    M = z.shape[0]
    T = seq_len
    DH = 128
    E = H_ATT * DH
    W = kc.shape[2]
    cache_spec = pl.BlockSpec((None, None, W, H_ATT, DH), lambda b: (0, b, 0, 0, 0))
    new_spec = pl.BlockSpec((None, None, T, H_ATT, DH), lambda b: (0, b, 0, 0, 0))
    new_shape = jax.ShapeDtypeStruct((1, n_seq, T, H_ATT, DH), F32)
    return pl.pallas_call(
        _attn_sample_kernel, grid=(n_seq,),
        in_specs=[pl.BlockSpec((T, E), lambda b: (b, 0)),
                  pl.BlockSpec((T, E), lambda b: (b, 1)),
                  pl.BlockSpec((T, E), lambda b: (b, 2)),
                  cache_spec, cache_spec,
                  pl.BlockSpec((H_ATT, T, K_WIN), lambda b: (0, 0, 0))],
        out_specs=[pl.BlockSpec((T, E), lambda b: (b, 0)), new_spec, new_spec],
        out_shape=[jax.ShapeDtypeStruct((M, E), BF16), new_shape, new_shape],
        compiler_params=_cparams(("parallel",)), name="attn_sample")(z, z, z, kc, vc, bias)


def _rope_tables(pos):
    half = LANES
    inv = ROPE_BASE ** (-jnp.arange(half, dtype=F32) / half)
    ang = pos.astype(F32)[:, None] * inv[None, :]
    return jnp.cos(ang), jnp.sin(ang)


def _trunk(x, pos, wts, bias, *, n_seq, seq_len, state=None):
    D = x.shape[-1]
    M = n_seq * seq_len
    x = x.reshape(M, D)
    sample = state is not None
    chunk = min(CHUNK, seq_len)
    cos, sin = _rope_tables(pos)

    z, la = _norm_matmul(x, wts['norm_mix_g'][0], wts['w_in_main'], bm=MM_BM, bn=MM_BN,
                         gate=(wts['w_in_lo'], wts['gate_w2'], wts['gate_b']))
    if sample:
        o, s_ret, s_gla = _ab_mixer(z, la, cos, sin, wts['ret_g'], wts['gla_g'], n_seq=n_seq, seq_len=seq_len,
                                    chunk=chunk, rows_per_step=256, s_ret=state['ret'], s_gla=state['gla'])
    else:
        o, s_ret, s_gla = _ab_mixer(z, la, cos, sin, wts['ret_g'], wts['gla_g'], n_seq=n_seq, seq_len=seq_len,
                                    chunk=chunk, rows_per_step=256)
    x = _proj_residual(x, o, wts['w_out_ab'], bm=PROJ_BM)
    hist0 = state['conv'][0] if sample else None
    x, tail0 = _conv_ffn(x, wts['norm_ffn_g'][0], wts['w_up'][0], wts['conv_w'][0], wts['conv_b'][0],
                         wts['w_down'][0], bm=FFN_BM, bf=FFN_BF, seq_len=seq_len, hist=hist0)

    z = _norm_matmul(x, wts['norm_mix_g'][1], wts['w_qkv'], bm=MM_BM, bn=MM_BN)
    if sample:
        o, new_k, new_v = _attn_sample(z, state['k'], state['v'], bias, n_seq=n_seq, seq_len=seq_len)
    else:
        o, new_k, new_v = _attn_prompt(z, bias, n_seq=n_seq, seq_len=seq_len)
    x = _proj_residual(x, o, wts['w_out_att'], bm=PROJ_BM)
    hist1 = state['conv'][1] if sample else None
    y, tail1 = _conv_ffn(x, wts['norm_ffn_g'][1], wts['w_up'][1], wts['conv_w'][1], wts['conv_b'][1],
                         wts['w_down'][1], bm=FFN_BM, bf=FFN_BF, seq_len=seq_len, hist=hist1,
                         final_g=wts['norm_final_g'])

    def conv_state(tail):
        F = tail.shape[-1]
        t = tail.reshape(n_seq, -1, SUBLANES, F)[:, -1]
        return t[:, SUBLANES - (CONV_W - 1):]

    conv = jnp.stack([conv_state(tail0), conv_state(tail1)])
    return (y.reshape(n_seq, seq_len, D), s_ret[None], s_gla[None], new_k, new_v, conv)


def kernel(x_prompt, x_sample, state_ret, state_gla, cache_attn_k, cache_attn_v, state_ffn_conv, norm_mix_g, w_in_ab, gla_gate_w2, gla_gate_b, ret_norm_g, gla_norm_g, w_out_ab, w_qkv_att, rel_bias_att, w_out_att, norm_ffn_g, w_ffn_up, ffn_conv_w, ffn_conv_b, w_ffn_down, norm_final_g):
    B, T, D = x_prompt.shape
    SB, ST, _ = x_sample.shape
    main = w_in_ab.shape[2] - GLA_GATE_RANK
    wts = dict(
        norm_mix_g=norm_mix_g[:, None, :],
        w_in_main=w_in_ab[0, :, :main].astype(BF16),
        w_in_lo=jnp.pad(w_in_ab[0, :, main:], ((0, 0), (0, LANES - GLA_GATE_RANK))).astype(BF16),
        gate_w2=jnp.pad(gla_gate_w2[0], ((0, LANES - GLA_GATE_RANK), (0, 0))).astype(BF16),
        gate_b=gla_gate_b[0][None, :],
        ret_g=ret_norm_g[0][None, :],
        gla_g=gla_norm_g[0][None, :],
        w_out_ab=w_out_ab[0].astype(BF16),
        w_qkv=w_qkv_att[0].astype(BF16),
        w_out_att=w_out_att[0].astype(BF16),
        norm_ffn_g=norm_ffn_g[:, None, :],
        w_up=w_ffn_up.astype(BF16),
        conv_w=ffn_conv_w,
        conv_b=ffn_conv_b[:, None, :],
        w_down=w_ffn_down.astype(BF16),
        norm_final_g=norm_final_g[None, :],
    )
    bias = _bias_tiles(rel_bias_att[0])
    state = dict(
        ret=state_ret[0], gla=state_gla[0],
        k=cache_attn_k, v=cache_attn_v,
        conv=jnp.pad(state_ffn_conv, ((0, 0), (0, 0), (SUBLANES - (CONV_W - 1), 0), (0, 0))),
    )
    outs_p = _trunk(x_prompt, jnp.arange(T), wts, bias, n_seq=B, seq_len=T)
    outs_s = _trunk(x_sample, PAST_LEN + jnp.arange(ST), wts, bias, n_seq=SB, seq_len=ST, state=state)
    return (outs_p[0], outs_s[0]) + outs_p[1:] + outs_s[1:]
```

```python
import functools
import math

import jax
import jax.numpy as jnp
import numpy as np
from jax import lax
from jax.experimental import pallas as pl
from jax.experimental.pallas import tpu as pltpu

F32 = jnp.float32
BF16 = jnp.bfloat16

CHUNK = 64
RMS_EPS = 1e-6
H_RET = 4
H_GLA = 4
GLA_GATE_RANK = 16
GLA_GATE_TAU = 16.0
ROPE_BASE = 10000.0
H_ATT = 16
LEFT_CHUNKS = 8
MAX_REL = 256
NEG_INF = -1e30
CONV_W = 3
PAST_LEN = 2048

LANES = 128
SUBLANES = 8
VMEM_LIMIT = 60 * 1024 * 1024

BAND = (LEFT_CHUNKS + 1) * CHUNK
Q_TILE = 2 * CHUNK
K_WIN = BAND + CHUNK
TOEP_W = 1024

MM_BM, MM_BN = 1024, 1024
PROJ_BM = 512
FFN_BM, FFN_BF = 1024, 512
ATT_GROUP = 4
ATT_STRIP = 32


def _cparams(sem):
    return pltpu.CompilerParams(dimension_semantics=sem, vmem_limit_bytes=VMEM_LIMIT)


def _rms(x, g):
    return x * lax.rsqrt(jnp.mean(x * x, axis=-1, keepdims=True) + RMS_EPS) * g


def _silu(x):
    return x * jax.nn.sigmoid(x)


def _dot(a, b):
    return jnp.dot(a, b, preferred_element_type=F32)


def _dot_nt(a, b):
    return lax.dot_general(a, b, (((1,), (1,)), ((), ())), preferred_element_type=F32)


def _dot_tn(a, b):
    return lax.dot_general(a, b, (((0,), (0,)), ((), ())), preferred_element_type=F32)


def _norm_mm_kernel(x_ref, g_ref, w_ref, o_ref, h_sc):
    @pl.when(pl.program_id(1) == 0)
    def _():
        h_sc[...] = _rms(x_ref[...], g_ref[...]).astype(BF16)

    o_ref[...] = _dot(h_sc[...], w_ref[...])


def _norm_mm_gate_kernel(x_ref, g_ref, w_ref, wlo_ref, w2_ref, gb_ref, o_ref, la_ref, h_sc):
    @pl.when(pl.program_id(1) == 0)
    def _():
        h = _rms(x_ref[...], g_ref[...]).astype(BF16)
        h_sc[...] = h
        lo = _dot(h, wlo_ref[...])
        xg = _dot(lo.astype(BF16), w2_ref[...]) + gb_ref[...]
        la_ref[...] = jax.nn.log_sigmoid(xg) / GLA_GATE_TAU

    o_ref[...] = _dot(h_sc[...], w_ref[...])


def _norm_matmul(x, g, w, *, bm, bn, gate=None, n_cols=None):
    M, D = x.shape
    N = w.shape[1] if n_cols is None else n_cols
    bm = min(bm, M)
    grid = (M // bm, N // bn)
    x_spec = pl.BlockSpec((bm, D), lambda i, j: (i, 0))
    g_spec = pl.BlockSpec((1, D), lambda i, j: (0, 0))
    w_spec = pl.BlockSpec((D, bn), lambda i, j: (0, j))
    o_spec = pl.BlockSpec((bm, bn), lambda i, j: (i, j))
    scratch = [pltpu.VMEM((bm, D), BF16)]
    if gate is None:
        return pl.pallas_call(
            _norm_mm_kernel, grid=grid, in_specs=[x_spec, g_spec, w_spec], out_specs=o_spec,
            out_shape=jax.ShapeDtypeStruct((M, N), F32), scratch_shapes=scratch,
            compiler_params=_cparams(("parallel", "arbitrary")), name="norm_matmul")(x, g, w)
    wlo, w2, gb = gate
    G = w2.shape[1]
    return pl.pallas_call(
        _norm_mm_gate_kernel, grid=grid,
        in_specs=[x_spec, g_spec, w_spec,
                  pl.BlockSpec(wlo.shape, lambda i, j: (0, 0)),
                  pl.BlockSpec(w2.shape, lambda i, j: (0, 0)),
                  pl.BlockSpec((1, G), lambda i, j: (0, 0))],
        out_specs=[o_spec, pl.BlockSpec((bm, G), lambda i, j: (i, 0))],
        out_shape=[jax.ShapeDtypeStruct((M, N), F32), jax.ShapeDtypeStruct((M, G), F32)],
        scratch_shapes=scratch,
        compiler_params=_cparams(("parallel", "arbitrary")), name="norm_matmul_gate")(x, g, w, wlo, w2, gb)


def _proj_res_kernel(x_ref, o_ref, w_ref, y_ref):
    y_ref[...] = x_ref[...] + _dot(o_ref[...], w_ref[...])


def _proj_residual(x, o, w, *, bm):
    M, D = x.shape
    K = o.shape[1]
    bm = min(bm, M)
    return pl.pallas_call(
        _proj_res_kernel, grid=(M // bm,),
        in_specs=[pl.BlockSpec((bm, D), lambda i: (i, 0)),
                  pl.BlockSpec((bm, K), lambda i: (i, 0)),
                  pl.BlockSpec((K, D), lambda i: (0, 0))],
        out_specs=pl.BlockSpec((bm, D), lambda i: (i, 0)),
        out_shape=jax.ShapeDtypeStruct((M, D), F32),
        compiler_params=_cparams(("parallel",)), name="proj_residual")(x, o, w)


def _ffn_kernel(*refs, nseq, seq_rows, tiles_per_seq, final_norm):
    refs = list(refs)
    x_ref, g_ref, wg_ref, wu_ref, cw_ref, cb_ref, wd_ref = refs[:7]
    pos = 7
    hist_ref = None
    if tiles_per_seq is None:
        hist_ref = refs[pos]
        pos += 1
    gf_ref = None
    if final_norm:
        gf_ref = refs[pos]
        pos += 1
    y_ref, tail_ref = refs[pos:pos + 2]
    h_sc, gbuf_sc = refs[pos + 2:pos + 4]
    carry_sc = refs[pos + 4] if tiles_per_seq is not None else None

    m = pl.program_id(0)
    f = pl.program_id(1)
    nf = pl.num_programs(1)
    L = seq_rows
    H = SUBLANES

    @pl.when(f == 0)
    def _():
        x = x_ref[...]
        h_sc[...] = _rms(x, g_ref[...]).astype(BF16)
        y_ref[...] = x

    h = h_sc[...]
    gate = _dot(h, wg_ref[...])
    up = _dot(h, wu_ref[...])
    bm, bf = gate.shape
    g3 = gate.reshape(nseq, L, bf)
    gbuf_sc[:, H:H + L, :] = g3
    if tiles_per_seq is None:
        gbuf_sc[:, 0:H, :] = hist_ref[...]
    else:
        first = (m % tiles_per_seq) == 0

        @pl.when(first)
        def _():
            gbuf_sc[:, 0:H, :] = jnp.zeros((nseq, H, bf), F32)

        @pl.when(jnp.logical_not(first))
        def _():
            gbuf_sc[:, 0:H, :] = carry_sc[f]

        carry_sc[f] = g3[:, L - H:L, :]
    tail_ref[...] = g3[:, L - H:L, :]

    g1 = gbuf_sc[:, H - 1:H - 1 + L, :]
    g2 = gbuf_sc[:, H - 2:H - 2 + L, :]
    cw = cw_ref[...]
    gc = cb_ref[...] + g2 * cw[0:1, :]
    gc = gc + g1 * cw[1:2, :]
    gc = gc + g3 * cw[2:3, :]
    act = (_silu(gc) * up.reshape(nseq, L, bf)).reshape(bm, bf).astype(BF16)
    y_ref[...] += _dot(act, wd_ref[...])

    if final_norm:
        @pl.when(f == nf - 1)
        def _():
            y_ref[...] = _rms(y_ref[...], gf_ref[...])


def _conv_ffn(x, g, w_up, cw, cb, wd, *, layer, bm, bf, seq_len, hist=None, final_g=None):
    M, D = x.shape
    F = wd.shape[1]
    bm = min(bm, M)
    nf = F // bf
    if seq_len >= bm:
        nseq, seq_rows, tiles_per_seq = 1, bm, seq_len // bm
        n_groups = M // bm
    else:
        nseq, seq_rows, tiles_per_seq = bm // seq_len, seq_len, None
        n_groups = M // seq_len
    in_specs = [pl.BlockSpec((bm, D), lambda i, j: (i, 0)),
                pl.BlockSpec((1, D), lambda i, j: (0, 0)),
                pl.BlockSpec((None, D, bf), lambda i, j: (layer, 0, j)),
                pl.BlockSpec((None, D, bf), lambda i, j: (layer, 0, nf + j)),
                pl.BlockSpec((CONV_W, bf), lambda i, j: (0, j)),
                pl.BlockSpec((1, bf), lambda i, j: (0, j)),
                pl.BlockSpec((None, bf, D), lambda i, j: (layer, j, 0))]
    args = [x, g, w_up, w_up, cw, cb, wd]
    if tiles_per_seq is None:
        in_specs.append(pl.BlockSpec((nseq, SUBLANES, bf), lambda i, j: (i, 0, j)))
        args.append(hist)
    if final_g is not None:
        in_specs.append(pl.BlockSpec((1, D), lambda i, j: (0, 0)))
        args.append(final_g)
    scratch = [pltpu.VMEM((bm, D), BF16), pltpu.VMEM((nseq, SUBLANES + seq_rows, bf), F32)]
    if tiles_per_seq is not None:
        scratch.append(pltpu.VMEM((nf, 1, SUBLANES, bf), F32))
    kern = functools.partial(_ffn_kernel, nseq=nseq, seq_rows=seq_rows, tiles_per_seq=tiles_per_seq,
                             final_norm=final_g is not None)
    return pl.pallas_call(
        kern, grid=(M // bm, nf), in_specs=in_specs,
        out_specs=[pl.BlockSpec((bm, D), lambda i, j: (i, 0)),
                   pl.BlockSpec((nseq, SUBLANES, bf), lambda i, j: (i, 0, j))],
        out_shape=[jax.ShapeDtypeStruct((M, D), F32), jax.ShapeDtypeStruct((n_groups, SUBLANES, F), F32)],
        scratch_shapes=scratch,
        compiler_params=_cparams(("arbitrary", "arbitrary")), name="conv_ffn")(*args)


def _ab_kernel(*refs, C, n_sub, has_state):
    refs = list(refs)
    z_ref, la_ref, cos_ref, sin_ref, rg_ref, gg_ref = refs[:6]
    pos = 6
    if has_state:
        sr0_ref, sg0_ref = refs[pos:pos + 2]
        pos += 2
    o_ref, sr_out_ref, sg_out_ref = refs[pos:pos + 3]
    sret_sc, sgla_sc = refs[pos + 3:pos + 5]

    t = pl.program_id(1)
    nt = pl.num_programs(1)
    DK_R = 256
    DV = 256
    DK_G = 128

    @pl.when(t == 0)
    def _():
        if has_state:
            sret_sc[...] = sr0_ref[0]
            sgla_sc[...] = sg0_ref[0]
        else:
            sret_sc[...] = jnp.zeros(sret_sc.shape, F32)
            sgla_sc[...] = jnp.zeros(sgla_sc.shape, F32)

    row = lax.broadcasted_iota(jnp.int32, (C, C), 0)
    col = lax.broadcasted_iota(jnp.int32, (C, C), 1)
    causal = row >= col
    diff = jnp.where(causal, row - col, 0).astype(F32)
    tri = jnp.where(causal, 1.0, 0.0).astype(BF16)
    ridx = lax.broadcasted_iota(jnp.int32, (C, 1), 0).astype(F32)
    SB = 2 * SUBLANES
    nsb = C // SB
    sub3 = lax.broadcasted_iota(jnp.int32, (nsb, SB, DK_G), 1)
    lane_c = lax.broadcasted_iota(jnp.int32, (C, LANES), 1)
    row_c = lax.broadcasted_iota(jnp.int32, (C, LANES), 0)
    blk0_c = (row_c // SB) * SB
    ones = jnp.ones((LANES, LANES), BF16)
    rg = rg_ref[...]
    gg = gg_ref[...]

    def chunk(c, carry):
        r0 = pl.multiple_of(c * C, C)
        rows = pl.ds(r0, C)
        cos = cos_ref[rows, :]
        sin = sin_ref[rows, :]

        def rope(x):
            x1 = x[:, :LANES]
            x2 = x[:, LANES:]
            return jnp.concatenate([x1 * cos - x2 * sin, x1 * sin + x2 * cos], axis=-1)

        outs, new_sret, new_sgla = [], [], []

        for h in range(H_RET):
            lg = math.log1p(-2.0 ** (-5.0 - h))
            q = rope(z_ref[rows, pl.ds(h * DK_R, DK_R)])
            k = rope(z_ref[rows, pl.ds(H_RET * DK_R + h * DK_R, DK_R)]) * (DK_R ** -0.5)
            v = z_ref[rows, pl.ds(2 * H_RET * DK_R + h * DV, DV)].astype(BF16)
            gate = z_ref[rows, pl.ds(2 * H_RET * DK_R + H_RET * DV + h * DV, DV)]
            S = sret_sc[h]
            decay = jnp.where(causal, jnp.exp(diff * lg), 0.0)
            scores = _dot_nt(q.astype(BF16), k.astype(BF16)) * decay
            o = _dot(scores.astype(BF16), v)
            q_dec = jnp.exp((ridx + 1.0) * lg)
            o = o + _dot((q * q_dec).astype(BF16), S.astype(BF16))
            k_dec = jnp.exp((C - 1.0 - ridx) * lg)
            new_sret.append(math.exp(C * lg) * S + _dot_tn((k * k_dec).astype(BF16), v))
            outs.append((_rms(o, rg) * _silu(gate)).astype(o_ref.dtype))

        base = 2 * H_RET * DK_R + 2 * H_RET * DV
        for h in range(H_GLA):
            q = z_ref[rows, pl.ds(base + h * DK_G, DK_G)] * (DK_G ** -0.5)
            k = z_ref[rows, pl.ds(base + H_GLA * DK_G + h * DK_G, DK_G)]
            v = z_ref[rows, pl.ds(base + 2 * H_GLA * DK_G + h * DV, DV)].astype(BF16)
            gate = z_ref[rows, pl.ds(base + 2 * H_GLA * DK_G + H_GLA * DV + h * DV, DV)]
            la = la_ref[rows, pl.ds(h * DK_G, DK_G)]
            la_hi = la.astype(BF16)
            la_lo = (la - la_hi.astype(F32)).astype(BF16)
            b2 = _dot(tri, jnp.concatenate([la_hi, la_lo], axis=-1))
            b = b2[:, :DK_G] + b2[:, DK_G:]
            S = sgla_sc[h]
            o = _dot((q * jnp.exp(b)).astype(BF16), S.astype(BF16))
            b_last = b[C - 1:C, :]
            kd = k * jnp.exp(b_last - b)
            e_col = jnp.transpose(jnp.broadcast_to(jnp.exp(b_last), (LANES, DK_G)))[:, 0:1]
            new_sgla.append(e_col * S + _dot_tn(kd.astype(BF16), v))
            q3 = q.reshape(nsb, SB, DK_G)
            k3 = k.reshape(nsb, SB, DK_G)
            b3 = b.reshape(nsb, SB, DK_G)
            pieces = []
            for j in range(SB):
                valid = sub3 >= j
                w = jnp.exp(jnp.where(valid, b3 - b3[:, j:j + 1, :], 0.0))
                pieces.append(jnp.where(valid, q3 * k3[:, j:j + 1, :] * w, 0.0).reshape(C, DK_G))
            sums = _dot(jnp.concatenate(pieces, axis=0).astype(BF16), ones)
            diag = jnp.zeros((C, LANES), F32)
            for j in range(SB):
                diag = jnp.where(lane_c == blk0_c + j, sums[j * C:(j + 1) * C, :], diag)
            blocks = [diag[0:SB, :C]]
            for I in range(1, nsb):
                i0 = I * SB
                b_r = b[i0:i0 + 1, :]
                qt = q[i0:i0 + SB, :] * jnp.exp(b[i0:i0 + SB, :] - b_r)
                before = row_c < i0
                kt = jnp.where(before, k * jnp.exp(jnp.where(before, b_r - b, 0.0)), 0.0)
                blocks.append(_dot_nt(qt.astype(BF16), kt.astype(BF16)) + diag[i0:i0 + SB, :C])
            scores = jnp.concatenate(blocks, axis=0)
            o = o + _dot(scores.astype(BF16), v)
            outs.append((_rms(o, gg) * _silu(gate)).astype(o_ref.dtype))
        o_ref[rows, :] = jnp.concatenate(outs, axis=-1)
        sret_sc[...] = jnp.stack(new_sret)
        sgla_sc[...] = jnp.stack(new_sgla)
        return carry

    lax.fori_loop(0, n_sub, chunk, 0)

    @pl.when(t == nt - 1)
    def _():
        sr_out_ref[0] = sret_sc[...]
        sg_out_ref[0] = sgla_sc[...]


def _ab_mixer(z, la, cos, sin, rg, gg, *, n_seq, seq_len, chunk, rows_per_step, s_ret=None, s_gla=None):
    M, ZW = z.shape
    R = min(rows_per_step, seq_len)
    nt = seq_len // R
    has_state = s_ret is not None
    in_specs = [pl.BlockSpec((R, ZW), lambda b, t: (b * nt + t, 0)),
                pl.BlockSpec((R, la.shape[1]), lambda b, t: (b * nt + t, 0)),
                pl.BlockSpec((R, LANES), lambda b, t: (t, 0)),
                pl.BlockSpec((R, LANES), lambda b, t: (t, 0)),
                pl.BlockSpec((1, 256), lambda b, t: (0, 0)),
                pl.BlockSpec((1, 256), lambda b, t: (0, 0))]
    args = [z, la, cos, sin, rg, gg]
    sr_spec = pl.BlockSpec((1, H_RET, 256, 256), lambda b, t: (b, 0, 0, 0))
    sg_spec = pl.BlockSpec((1, H_GLA, 128, 256), lambda b, t: (b, 0, 0, 0))
    if has_state:
        in_specs += [sr_spec, sg_spec]
        args += [s_ret, s_gla]
    kern = functools.partial(_ab_kernel, C=chunk, n_sub=R // chunk, has_state=has_state)
    return pl.pallas_call(
        kern, grid=(n_seq, nt), in_specs=in_specs,
        out_specs=[pl.BlockSpec((R, 2048), lambda b, t: (b * nt + t, 0)), sr_spec, sg_spec],
        out_shape=[jax.ShapeDtypeStruct((M, 2048), BF16),
                   jax.ShapeDtypeStruct((n_seq, H_RET, 256, 256), F32),
                   jax.ShapeDtypeStruct((n_seq, H_GLA, 128, 256), F32)],
        scratch_shapes=[pltpu.VMEM((H_RET, 256, 256), F32), pltpu.VMEM((H_GLA, 128, 256), F32)],
        compiler_params=_cparams(("parallel", "arbitrary")), name="ab_mixer")(*args)


def _bias_kernel(rb_ref, o_ref):
    NB = rb_ref.shape[1]
    u = lax.broadcasted_iota(jnp.int32, (NB, TOEP_W), 1)
    d = jnp.where(u < TOEP_W - Q_TILE, u, u - TOEP_W)
    idx = jnp.clip(LEFT_CHUNKS * CHUNK - d, -MAX_REL, MAX_REL) + MAX_REL
    mrow = lax.broadcasted_iota(jnp.int32, (NB, TOEP_W), 0)
    onehot = jnp.where(mrow == idx, 1.0, 0.0).astype(BF16)
    rb = rb_ref[...]
    hi = rb.astype(BF16)
    r1 = rb - hi.astype(F32)
    mid = r1.astype(BF16)
    lo = (r1 - mid.astype(F32)).astype(BF16)
    grow = _dot(hi, onehot) + _dot(mid, onehot) + _dot(lo, onehot)
    i = lax.broadcasted_iota(jnp.int32, (Q_TILE, K_WIN), 0)
    j = lax.broadcasted_iota(jnp.int32, (Q_TILE, K_WIN), 1)
    rel_chunk = j // CHUNK - i // CHUNK
    in_band = (rel_chunk >= 0) & (rel_chunk <= LEFT_CHUNKS)
    for h in range(rb_ref.shape[0]):
        g = jnp.broadcast_to(grow[h:h + 1, :], (Q_TILE, TOEP_W))
        toep = pltpu.roll(g, 0, 1, stride=1, stride_axis=0)[:, :K_WIN]
        o_ref[h] = jnp.where(in_band, toep, NEG_INF)


def _bias_tiles(rel_bias):
    H, NR = rel_bias.shape
    NB = 5 * LANES
    rbp = jnp.pad(rel_bias, ((0, 0), (0, NB - NR)))
    return pl.pallas_call(
        _bias_kernel, out_shape=jax.ShapeDtypeStruct((H, Q_TILE, K_WIN), F32),
        compiler_params=pltpu.CompilerParams(vmem_limit_bytes=VMEM_LIMIT), name="rel_bias_tiles")(rbp)


def _attn_prompt_kernel(q_ref, k_ref, v_ref, bias_ref, o_ref, ko_ref, vo_ref, kpad_sc, vpad_sc):
    T, DH = k_ref.shape
    KEEP = ko_ref.shape[0] // H_ATT
    PAD = LEFT_CHUNKS * CHUNK
    h = pl.program_id(1)
    k = k_ref[...]
    v = v_ref[...]
    ko_ref[pl.ds(h, KEEP, stride=H_ATT), :] = k[T - KEEP:, :]
    vo_ref[pl.ds(h, KEEP, stride=H_ATT), :] = v[T - KEEP:, :]
    kpad_sc[0:PAD, :] = jnp.zeros((PAD, DH), BF16)
    vpad_sc[0:PAD, :] = jnp.zeros((PAD, DH), BF16)
    kpad_sc[PAD:PAD + T, :] = k.astype(BF16)
    vpad_sc[PAD:PAD + T, :] = v.astype(BF16)
    scale = DH ** -0.5
    colj = lax.broadcasted_iota(jnp.int32, (ATT_STRIP, K_WIN), 1)

    def scores(r0):
        q = (q_ref[pl.ds(r0, Q_TILE), :] * scale).astype(BF16)
        return _dot_nt(q, kpad_sc[pl.ds(r0, K_WIN), :])

    def attend(r0, s, masked):
        ps, dens = [], []
        for r in range(0, Q_TILE, ATT_STRIP):
            sr = s[r:r + ATT_STRIP, :] + bias_ref[0, r:r + ATT_STRIP, :]
            if masked:
                sr = jnp.where(colj + r0 >= PAD, sr, NEG_INF)
            mx = jnp.max(sr, axis=-1, keepdims=True)
            p = jnp.exp(sr - mx)
            dens.append(jnp.sum(p, axis=-1, keepdims=True))
            ps.append(p.astype(BF16))
        o = _dot(jnp.concatenate(ps, axis=0), vpad_sc[pl.ds(r0, K_WIN), :])
        o = [o[i * ATT_STRIP:(i + 1) * ATT_STRIP, :] / den for i, den in enumerate(dens)]
        return jnp.concatenate(o, axis=0).astype(o_ref.dtype)

    n_tiles = T // Q_TILE
    n_masked = min(PAD // Q_TILE, n_tiles)
    G = ATT_GROUP if (n_masked % ATT_GROUP == 0 and n_tiles % ATT_GROUP == 0) else 1

    def group(g, carry, *, masked):
        g0 = pl.multiple_of(g * (G * Q_TILE), G * Q_TILE)
        s_next = scores(g0)
        outs = []
        for t in range(G):
            s_cur = s_next
            if t + 1 < G:
                s_next = scores(g0 + (t + 1) * Q_TILE)
            outs.append(attend(g0 + t * Q_TILE, s_cur, masked))
        o_ref[pl.ds(g0, G * Q_TILE), :] = jnp.concatenate(outs, axis=0)
        return carry

    lax.fori_loop(0, n_masked // G, functools.partial(group, masked=True), 0)
    lax.fori_loop(n_masked // G, n_tiles // G, functools.partial(group, masked=False), 0)


def _attn_prompt(z, bias, *, n_seq, seq_len):
    M = z.shape[0]
    DH = 128
    T = seq_len
    keep = min(LEFT_CHUNKS * CHUNK, T)
    kv_spec = pl.BlockSpec((None, None, keep * H_ATT, DH), lambda b, h: (0, b, 0, 0))
    kv_shape = jax.ShapeDtypeStruct((1, n_seq, keep * H_ATT, DH), F32)
    return pl.pallas_call(
        _attn_prompt_kernel, grid=(n_seq, H_ATT),
        in_specs=[pl.BlockSpec((T, DH), lambda b, h: (b, h)),
                  pl.BlockSpec((T, DH), lambda b, h: (b, H_ATT + h)),
                  pl.BlockSpec((T, DH), lambda b, h: (b, 2 * H_ATT + h)),
                  pl.BlockSpec((1, Q_TILE, K_WIN), lambda b, h: (h, 0, 0))],
        out_specs=[pl.BlockSpec((T, DH), lambda b, h: (b, h)), kv_spec, kv_spec],
        out_shape=[jax.ShapeDtypeStruct((M, H_ATT * DH), BF16), kv_shape, kv_shape],
        scratch_shapes=[pltpu.VMEM((LEFT_CHUNKS * CHUNK + T, DH), BF16),
                        pltpu.VMEM((LEFT_CHUNKS * CHUNK + T, DH), BF16)],
        compiler_params=_cparams(("parallel", "arbitrary")), name="attn_prompt")(z, z, z, bias)


def _attn_sample_kernel(q_ref, k_ref, v_ref, kc_ref, vc_ref, bias_ref, o_ref, ko_ref, vo_ref):
    T = q_ref.shape[0]
    W = kc_ref.shape[0] // H_ATT
    DH = 128
    scale = DH ** -0.5
    outs = []
    for h in range(H_ATT):
        cs = pl.ds(h * DH, DH)
        q = (q_ref[:, cs] * scale).astype(BF16)
        kn = k_ref[:, cs]
        vn = v_ref[:, cs]
        ko_ref[pl.ds(h, T, stride=H_ATT), :] = kn
        vo_ref[pl.ds(h, T, stride=H_ATT), :] = vn
        kn = kn.astype(BF16)
        vn = vn.astype(BF16)
        kc = kc_ref[pl.ds(h, W, stride=H_ATT), :].astype(BF16)
        vc = vc_ref[pl.ds(h, W, stride=H_ATT), :].astype(BF16)
        s1 = _dot_nt(q, kc) + bias_ref[h, :, 0:W]
        s2 = _dot_nt(q, kn) + bias_ref[h, :, W:W + T]
        mx = jnp.maximum(jnp.max(s1, axis=-1, keepdims=True), jnp.max(s2, axis=-1, keepdims=True))
        p1 = jnp.exp(s1 - mx)
        p2 = jnp.exp(s2 - mx)
        den = jnp.sum(p1, axis=-1, keepdims=True) + jnp.sum(p2, axis=-1, keepdims=True)
        o = (_dot(p1.astype(BF16), vc) + _dot(p2.astype(BF16), vn)) / den
        outs.append(o.astype(o_ref.dtype))
    o_ref[...] = jnp.concatenate(outs, axis=-1)


def _attn_sample(z, kc, vc, bias, *, n_seq, seq_len):
    M = z.shape[0]
    T = seq_len
    DH = 128
    E = H_ATT * DH
    WH = kc.shape[2]
    cache_spec = pl.BlockSpec((None, None, WH, DH), lambda b: (0, b, 0, 0))
    new_spec = pl.BlockSpec((None, None, T * H_ATT, DH), lambda b: (0, b, 0, 0))
    new_shape = jax.ShapeDtypeStruct((1, n_seq, T * H_ATT, DH), F32)
    return pl.pallas_call(
        _attn_sample_kernel, grid=(n_seq,),
        in_specs=[pl.BlockSpec((T, E), lambda b: (b, 0)),
                  pl.BlockSpec((T, E), lambda b: (b, 1)),
                  pl.BlockSpec((T, E), lambda b: (b, 2)),
                  cache_spec, cache_spec,
                  pl.BlockSpec((H_ATT, T, K_WIN), lambda b: (0, 0, 0))],
        out_specs=[pl.BlockSpec((T, E), lambda b: (b, 0)), new_spec, new_spec],
        out_shape=[jax.ShapeDtypeStruct((M, E), BF16), new_shape, new_shape],
        compiler_params=_cparams(("parallel",)), name="attn_sample")(z, z, z, kc, vc, bias)


def _rope_tables(pos):
    half = LANES
    inv = ROPE_BASE ** (-jnp.arange(half, dtype=F32) / half)
    ang = pos.astype(F32)[:, None] * inv[None, :]
    return jnp.cos(ang), jnp.sin(ang)


def _trunk(x, pos, wts, bias, *, n_seq, seq_len, state=None):
    D = x.shape[-1]
    M = n_seq * seq_len
    x = x.reshape(M, D)
    sample = state is not None
    chunk = min(CHUNK, seq_len)
    cos, sin = _rope_tables(pos)

    n_main = wts['w_in'].shape[1] - GLA_GATE_RANK
    z, la = _norm_matmul(x, wts['norm_mix_g'][0], wts['w_in'], bm=MM_BM, bn=MM_BN, n_cols=n_main,
                         gate=(wts['w_in_lo'], wts['gate_w2'], wts['gate_b']))
    if sample:
        o, s_ret, s_gla = _ab_mixer(z, la, cos, sin, wts['ret_g'], wts['gla_g'], n_seq=n_seq, seq_len=seq_len,
                                    chunk=chunk, rows_per_step=256, s_ret=state['ret'], s_gla=state['gla'])
    else:
        o, s_ret, s_gla = _ab_mixer(z, la, cos, sin, wts['ret_g'], wts['gla_g'], n_seq=n_seq, seq_len=seq_len,
                                    chunk=chunk, rows_per_step=256)
    x = _proj_residual(x, o, wts['w_out_ab'], bm=PROJ_BM)
    hist0 = state['conv'][0] if sample else None
    x, tail0 = _conv_ffn(x, wts['norm_ffn_g'][0], wts['w_up'], wts['conv_w'][0], wts['conv_b'][0],
                         wts['w_down'], layer=0, bm=FFN_BM, bf=FFN_BF, seq_len=seq_len, hist=hist0)

    z = _norm_matmul(x, wts['norm_mix_g'][1], wts['w_qkv'], bm=MM_BM, bn=MM_BN)
    if sample:
        o, new_k, new_v = _attn_sample(z, state['k'], state['v'], bias, n_seq=n_seq, seq_len=seq_len)
    else:
        o, new_k, new_v = _attn_prompt(z, bias, n_seq=n_seq, seq_len=seq_len)
    x = _proj_residual(x, o, wts['w_out_att'], bm=PROJ_BM)
    hist1 = state['conv'][1] if sample else None
    y, tail1 = _conv_ffn(x, wts['norm_ffn_g'][1], wts['w_up'], wts['conv_w'][1], wts['conv_b'][1],
                         wts['w_down'], layer=1, bm=FFN_BM, bf=FFN_BF, seq_len=seq_len, hist=hist1,
                         final_g=wts['norm_final_g'])

    def conv_state(tail):
        F = tail.shape[-1]
        t = tail.reshape(n_seq, -1, SUBLANES, F)[:, -1]
        return t[:, SUBLANES - (CONV_W - 1):]

    conv = jnp.stack([conv_state(tail0), conv_state(tail1)])
    kv_shape = (1, n_seq, -1, H_ATT, 128)
    return (y.reshape(n_seq, seq_len, D), s_ret[None], s_gla[None], new_k.reshape(kv_shape), new_v.reshape(kv_shape),
            conv)


def kernel(x_prompt, x_sample, state_ret, state_gla, cache_attn_k, cache_attn_v, state_ffn_conv, norm_mix_g, w_in_ab, gla_gate_w2, gla_gate_b, ret_norm_g, gla_norm_g, w_out_ab, w_qkv_att, rel_bias_att, w_out_att, norm_ffn_g, w_ffn_up, ffn_conv_w, ffn_conv_b, w_ffn_down, norm_final_g):
    B, T, D = x_prompt.shape
    SB, ST, _ = x_sample.shape
    main = w_in_ab.shape[2] - GLA_GATE_RANK
    wts = dict(
        norm_mix_g=norm_mix_g[:, None, :],
        w_in=w_in_ab[0].astype(BF16),
        w_in_lo=jnp.pad(w_in_ab[0, :, main:], ((0, 0), (0, LANES - GLA_GATE_RANK))).astype(BF16),
        gate_w2=jnp.pad(gla_gate_w2[0], ((0, LANES - GLA_GATE_RANK), (0, 0))).astype(BF16),
        gate_b=gla_gate_b[0][None, :],
        ret_g=ret_norm_g[0][None, :],
        gla_g=gla_norm_g[0][None, :],
        w_out_ab=w_out_ab[0].astype(BF16),
        w_qkv=w_qkv_att[0].astype(BF16),
        w_out_att=w_out_att[0].astype(BF16),
        norm_ffn_g=norm_ffn_g[:, None, :],
        w_up=w_ffn_up.astype(BF16),
        conv_w=ffn_conv_w,
        conv_b=ffn_conv_b[:, None, :],
        w_down=w_ffn_down.astype(BF16),
        norm_final_g=norm_final_g[None, :],
    )
    bias = _bias_tiles(rel_bias_att[0])
    state = dict(
        ret=state_ret[0], gla=state_gla[0],
        k=cache_attn_k.reshape(cache_attn_k.shape[:2] + (-1, cache_attn_k.shape[-1])),
        v=cache_attn_v.reshape(cache_attn_v.shape[:2] + (-1, cache_attn_v.shape[-1])),
        conv=jnp.pad(state_ffn_conv, ((0, 0), (0, 0), (SUBLANES - (CONV_W - 1), 0), (0, 0))),
    )
    outs_p = _trunk(x_prompt, jnp.arange(T), wts, bias, n_seq=B, seq_len=T)
    outs_s = _trunk(x_sample, PAST_LEN + jnp.arange(ST), wts, bias, n_seq=SB, seq_len=ST, state=state)
    return (outs_p[0], outs_s[0]) + outs_p[1:] + outs_s[1:]
```

```python
import functools
import math

import jax
import jax.numpy as jnp
import numpy as np
from jax import lax
from jax.experimental import pallas as pl
from jax.experimental.pallas import tpu as pltpu

F32 = jnp.float32
BF16 = jnp.bfloat16

CHUNK = 64
RMS_EPS = 1e-6
H_RET = 4
H_GLA = 4
GLA_GATE_RANK = 16
GLA_GATE_TAU = 16.0
ROPE_BASE = 10000.0
H_ATT = 16
LEFT_CHUNKS = 8
MAX_REL = 256
NEG_INF = -1e30
CONV_W = 3
PAST_LEN = 2048

LANES = 128
SUBLANES = 8
VMEM_LIMIT = 60 * 1024 * 1024

BAND = (LEFT_CHUNKS + 1) * CHUNK
Q_TILE = 2 * CHUNK
K_WIN = BAND + CHUNK
TOEP_W = 1024

MM_BM, MM_BN = 1024, 1024
PROJ_BM = 512
FFN_BM, FFN_BF = 1024, 512
FFN_SPLIT = 2
ATT_GROUP = 4
ATT_STRIP = 32


def _cparams(sem):
    return pltpu.CompilerParams(dimension_semantics=sem, vmem_limit_bytes=VMEM_LIMIT)


def _rms(x, g):
    return x * lax.rsqrt(jnp.mean(x * x, axis=-1, keepdims=True) + RMS_EPS) * g


def _silu(x):
    return x * jax.nn.sigmoid(x)


def _dot(a, b):
    return jnp.dot(a, b, preferred_element_type=F32)


def _dot_nt(a, b):
    return lax.dot_general(a, b, (((1,), (1,)), ((), ())), preferred_element_type=F32)


def _dot_tn(a, b):
    return lax.dot_general(a, b, (((0,), (0,)), ((), ())), preferred_element_type=F32)


def _norm_mm_kernel(x_ref, g_ref, w_ref, o_ref, h_sc):
    @pl.when(pl.program_id(1) == 0)
    def _():
        h_sc[...] = _rms(x_ref[...], g_ref[...]).astype(BF16)

    o_ref[...] = _dot(h_sc[...], w_ref[...])


def _norm_mm_gate_kernel(x_ref, g_ref, w_ref, wlo_ref, w2_ref, gb_ref, o_ref, la_ref, h_sc):
    @pl.when(pl.program_id(1) == 0)
    def _():
        h = _rms(x_ref[...], g_ref[...]).astype(BF16)
        h_sc[...] = h
        lo = _dot(h, wlo_ref[...])
        xg = _dot(lo.astype(BF16), w2_ref[...]) + gb_ref[...]
        la_ref[...] = jax.nn.log_sigmoid(xg) / GLA_GATE_TAU

    o_ref[...] = _dot(h_sc[...], w_ref[...])


def _norm_matmul(x, g, w, *, bm, bn, gate=None, n_cols=None):
    M, D = x.shape
    N = w.shape[1] if n_cols is None else n_cols
    bm = min(bm, M)
    grid = (M // bm, N // bn)
    x_spec = pl.BlockSpec((bm, D), lambda i, j: (i, 0))
    g_spec = pl.BlockSpec((1, D), lambda i, j: (0, 0))
    w_spec = pl.BlockSpec((D, bn), lambda i, j: (0, j))
    o_spec = pl.BlockSpec((bm, bn), lambda i, j: (i, j))
    scratch = [pltpu.VMEM((bm, D), BF16)]
    if gate is None:
        return pl.pallas_call(
            _norm_mm_kernel, grid=grid, in_specs=[x_spec, g_spec, w_spec], out_specs=o_spec,
            out_shape=jax.ShapeDtypeStruct((M, N), F32), scratch_shapes=scratch,
            compiler_params=_cparams(("parallel", "arbitrary")), name="norm_matmul")(x, g, w)
    wlo, w2, gb = gate
    G = w2.shape[1]
    return pl.pallas_call(
        _norm_mm_gate_kernel, grid=grid,
        in_specs=[x_spec, g_spec, w_spec,
                  pl.BlockSpec(wlo.shape, lambda i, j: (0, 0)),
                  pl.BlockSpec(w2.shape, lambda i, j: (0, 0)),
                  pl.BlockSpec((1, G), lambda i, j: (0, 0))],
        out_specs=[o_spec, pl.BlockSpec((bm, G), lambda i, j: (i, 0))],
        out_shape=[jax.ShapeDtypeStruct((M, N), F32), jax.ShapeDtypeStruct((M, G), F32)],
        scratch_shapes=scratch,
        compiler_params=_cparams(("parallel", "arbitrary")), name="norm_matmul_gate")(x, g, w, wlo, w2, gb)


def _proj_res_kernel(x_ref, o_ref, w_ref, y_ref):
    y_ref[...] = x_ref[...] + _dot(o_ref[...], w_ref[...])


def _proj_residual(x, o, w, *, bm):
    M, D = x.shape
    K = o.shape[1]
    bm = min(bm, M)
    return pl.pallas_call(
        _proj_res_kernel, grid=(M // bm,),
        in_specs=[pl.BlockSpec((bm, D), lambda i: (i, 0)),
                  pl.BlockSpec((bm, K), lambda i: (i, 0)),
                  pl.BlockSpec((K, D), lambda i: (0, 0))],
        out_specs=pl.BlockSpec((bm, D), lambda i: (i, 0)),
        out_shape=jax.ShapeDtypeStruct((M, D), F32),
        compiler_params=_cparams(("parallel",)), name="proj_residual")(x, o, w)


def _ffn_kernel(*refs, nseq, seq_rows, tiles_per_seq, final_norm):
    refs = list(refs)
    x_ref, g_ref, wg_ref, wu_ref, cw_ref, cb_ref, wd_ref = refs[:7]
    pos = 7
    hist_ref = None
    if tiles_per_seq is None:
        hist_ref = refs[pos]
        pos += 1
    gf_ref = None
    if final_norm:
        gf_ref = refs[pos]
        pos += 1
    y_ref, tail_ref = refs[pos:pos + 2]
    h_sc, gbuf_sc = refs[pos + 2:pos + 4]
    carry_sc = refs[pos + 4] if tiles_per_seq is not None else None

    m = pl.program_id(0)
    f = pl.program_id(1)
    nf = pl.num_programs(1)
    L = seq_rows
    H = SUBLANES

    @pl.when(f == 0)
    def _():
        x = x_ref[...]
        h_sc[...] = _rms(x, g_ref[...]).astype(BF16)
        y_ref[...] = x
        if carry_sc is not None:
            @pl.when(m == 0)
            def _():
                carry_sc[...] = jnp.zeros(carry_sc.shape, F32)

    h = h_sc[...]
    bm = h.shape[0]
    bf = wg_ref.shape[1]
    hw = bf // FFN_SPLIT
    gates = [_dot(h, wg_ref[:, c * hw:(c + 1) * hw]) for c in range(FFN_SPLIT)]
    ups = [_dot(h, wu_ref[:, c * hw:(c + 1) * hw]) for c in range(FFN_SPLIT)]
    if tiles_per_seq is None:
        hist = hist_ref[...]
    else:
        hist = jnp.where((m % tiles_per_seq) == 0, 0.0, carry_sc[f])
    acc = None
    tails = []
    for c in range(FFN_SPLIT):
        cs = slice(c * hw, (c + 1) * hw)
        g3 = gates[c].reshape(nseq, L, hw)
        gbuf_sc[:, H:H + L, cs] = g3
        gbuf_sc[:, 0:H, cs] = hist[:, :, cs]
        tails.append(g3[:, L - H:L, :])
        g1 = gbuf_sc[:, H - 1:H - 1 + L, cs]
        g2 = gbuf_sc[:, H - 2:H - 2 + L, cs]
        gc = cb_ref[:, cs] + g2 * cw_ref[0:1, cs]
        gc = gc + g1 * cw_ref[1:2, cs]
        gc = gc + g3 * cw_ref[2:3, cs]
        act = (_silu(gc) * ups[c].reshape(nseq, L, hw)).reshape(bm, hw).astype(BF16)
        d = _dot(act, wd_ref[cs, :])
        acc = d if acc is None else acc + d
    tail = jnp.concatenate(tails, axis=-1)
    tail_ref[...] = tail
    if carry_sc is not None:
        carry_sc[f] = tail
    y_ref[...] += acc

    if final_norm:
        @pl.when(f == nf - 1)
        def _():
            y_ref[...] = _rms(y_ref[...], gf_ref[...])


def _conv_ffn(x, g, w_up, cw, cb, wd, *, layer, bm, bf, seq_len, hist=None, final_g=None):
    M, D = x.shape
    F = wd.shape[1]
    bm = min(bm, M)
    nf = F // bf
    if seq_len >= bm:
        nseq, seq_rows, tiles_per_seq = 1, bm, seq_len // bm
        n_groups = M // bm
    else:
        nseq, seq_rows, tiles_per_seq = bm // seq_len, seq_len, None
        n_groups = M // seq_len
    in_specs = [pl.BlockSpec((bm, D), lambda i, j: (i, 0)),
                pl.BlockSpec((1, D), lambda i, j: (0, 0)),
                pl.BlockSpec((None, D, bf), lambda i, j: (layer, 0, j)),
                pl.BlockSpec((None, D, bf), lambda i, j: (layer, 0, nf + j)),
                pl.BlockSpec((CONV_W, bf), lambda i, j: (0, j)),
                pl.BlockSpec((1, bf), lambda i, j: (0, j)),
                pl.BlockSpec((None, bf, D), lambda i, j: (layer, j, 0))]
    args = [x, g, w_up, w_up, cw, cb, wd]
    if tiles_per_seq is None:
        in_specs.append(pl.BlockSpec((nseq, SUBLANES, bf), lambda i, j: (i, 0, j)))
        args.append(hist)
    if final_g is not None:
        in_specs.append(pl.BlockSpec((1, D), lambda i, j: (0, 0)))
        args.append(final_g)
    scratch = [pltpu.VMEM((bm, D), BF16), pltpu.VMEM((nseq, SUBLANES + seq_rows, bf), F32)]
    if tiles_per_seq is not None:
        scratch.append(pltpu.VMEM((nf, 1, SUBLANES, bf), F32))
    kern = functools.partial(_ffn_kernel, nseq=nseq, seq_rows=seq_rows, tiles_per_seq=tiles_per_seq,
                             final_norm=final_g is not None)
    return pl.pallas_call(
        kern, grid=(M // bm, nf), in_specs=in_specs,
        out_specs=[pl.BlockSpec((bm, D), lambda i, j: (i, 0)),
                   pl.BlockSpec((nseq, SUBLANES, bf), lambda i, j: (i, 0, j))],
        out_shape=[jax.ShapeDtypeStruct((M, D), F32), jax.ShapeDtypeStruct((n_groups, SUBLANES, F), F32)],
        scratch_shapes=scratch,
        compiler_params=_cparams(("arbitrary", "arbitrary")), name="conv_ffn")(*args)


def _ab_kernel(*refs, C, n_sub, has_state):
    refs = list(refs)
    z_ref, la_ref, cos_ref, sin_ref, rg_ref, gg_ref = refs[:6]
    pos = 6
    if has_state:
        sr0_ref, sg0_ref = refs[pos:pos + 2]
        pos += 2
    o_ref, sr_out_ref, sg_out_ref = refs[pos:pos + 3]
    sret_sc, sgla_sc = refs[pos + 3:pos + 5]

    t = pl.program_id(1)
    nt = pl.num_programs(1)
    DK_R = 256
    DV = 256
    DK_G = 128

    @pl.when(t == 0)
    def _():
        if has_state:
            sret_sc[...] = sr0_ref[0]
            sgla_sc[...] = sg0_ref[0]
        else:
            sret_sc[...] = jnp.zeros(sret_sc.shape, F32)
            sgla_sc[...] = jnp.zeros(sgla_sc.shape, F32)

    row = lax.broadcasted_iota(jnp.int32, (C, C), 0)
    col = lax.broadcasted_iota(jnp.int32, (C, C), 1)
    causal = row >= col
    diff = jnp.where(causal, row - col, 0).astype(F32)
    tri = jnp.where(causal, 1.0, 0.0).astype(BF16)
    ridx = lax.broadcasted_iota(jnp.int32, (C, 1), 0).astype(F32)
    SB = SUBLANES
    nsb = C // SB
    lane_c = lax.broadcasted_iota(jnp.int32, (C, LANES), 1)
    row_c = lax.broadcasted_iota(jnp.int32, (C, LANES), 0)
    blk0_c = (row_c // SB) * SB
    ones = jnp.ones((LANES, LANES), BF16)
    rg = rg_ref[...]
    gg = gg_ref[...]

    def chunk(c, carry):
        r0 = pl.multiple_of(c * C, C)
        rows = pl.ds(r0, C)
        cos = cos_ref[rows, :]
        sin = sin_ref[rows, :]

        def rope(x):
            x1 = x[:, :LANES]
            x2 = x[:, LANES:]
            return jnp.concatenate([x1 * cos - x2 * sin, x1 * sin + x2 * cos], axis=-1)

        def ret_head(h):
            lg = math.log1p(-2.0 ** (-5.0 - h))
            q = rope(z_ref[rows, pl.ds(h * DK_R, DK_R)])
            k = rope(z_ref[rows, pl.ds(H_RET * DK_R + h * DK_R, DK_R)]) * (DK_R ** -0.5)
            v = z_ref[rows, pl.ds(2 * H_RET * DK_R + h * DV, DV)].astype(BF16)
            gate = z_ref[rows, pl.ds(2 * H_RET * DK_R + H_RET * DV + h * DV, DV)]
            S = sret_sc[h]
            decay = jnp.where(causal, jnp.exp(diff * lg), 0.0)
            scores = _dot_nt(q.astype(BF16), k.astype(BF16)) * decay
            o = _dot(scores.astype(BF16), v)
            q_dec = jnp.exp((ridx + 1.0) * lg)
            o = o + _dot((q * q_dec).astype(BF16), S.astype(BF16))
            k_dec = jnp.exp((C - 1.0 - ridx) * lg)
            s_new = math.exp(C * lg) * S + _dot_tn((k * k_dec).astype(BF16), v)
            return (_rms(o, rg) * _silu(gate)).astype(o_ref.dtype), s_new

        base = 2 * H_RET * DK_R + 2 * H_RET * DV

        def gla_head(h):
            q = z_ref[rows, pl.ds(base + h * DK_G, DK_G)] * (DK_G ** -0.5)
            k = z_ref[rows, pl.ds(base + H_GLA * DK_G + h * DK_G, DK_G)]
            v = z_ref[rows, pl.ds(base + 2 * H_GLA * DK_G + h * DV, DV)].astype(BF16)
            gate = z_ref[rows, pl.ds(base + 2 * H_GLA * DK_G + H_GLA * DV + h * DV, DV)]
            la = la_ref[rows, pl.ds(h * DK_G, DK_G)]
            la_hi = la.astype(BF16)
            la_lo = (la - la_hi.astype(F32)).astype(BF16)
            b2 = _dot(tri, jnp.concatenate([la_hi, la_lo], axis=-1))
            b = b2[:, :DK_G] + b2[:, DK_G:]
            S = sgla_sc[h]
            o = _dot((q * jnp.exp(b)).astype(BF16), S.astype(BF16))
            b_last = b[C - 1:C, :]
            kd = k * jnp.exp(b_last - b)
            e_col = jnp.transpose(jnp.broadcast_to(jnp.exp(b_last), (LANES, DK_G)))[:, 0:1]
            s_new = e_col * S + _dot_tn(kd.astype(BF16), v)
            q3 = q.reshape(nsb, SB, DK_G)
            k3 = k.reshape(nsb, SB, DK_G)
            b3 = b.reshape(nsb, SB, DK_G)
            pieces = []
            for j in range(SB):
                w = jnp.exp(jnp.minimum(b3 - b3[:, j:j + 1, :], 0.0))
                pieces.append((q3 * k3[:, j:j + 1, :] * w).reshape(C, DK_G))
            sums = _dot(jnp.concatenate(pieces, axis=0).astype(BF16), ones)
            diag = jnp.zeros((C, LANES), F32)
            for j in range(SB):
                diag = jnp.where(lane_c == blk0_c + j, sums[j * C:(j + 1) * C, :], diag)
            diag = jnp.where(lane_c <= row_c, diag, 0.0)
            blocks = [diag[0:SB, :C]]
            for I in range(1, nsb):
                i0 = I * SB
                b_r = b[i0:i0 + 1, :]
                qt = q[i0:i0 + SB, :] * jnp.exp(b[i0:i0 + SB, :] - b_r)
                before = row_c < i0
                kt = jnp.where(before, k * jnp.exp(jnp.where(before, b_r - b, 0.0)), 0.0)
                blocks.append(_dot_nt(qt.astype(BF16), kt.astype(BF16)) + diag[i0:i0 + SB, :C])
            scores = jnp.concatenate(blocks, axis=0)
            o = o + _dot(scores.astype(BF16), v)
            return (_rms(o, gg) * _silu(gate)).astype(o_ref.dtype), s_new

        assert H_RET == H_GLA
        ret, gla = [], []
        for h in range(H_RET):
            ret.append(ret_head(h))
            gla.append(gla_head(h))
        o_ref[rows, :] = jnp.concatenate([o for o, _ in ret] + [o for o, _ in gla], axis=-1)
        sret_sc[...] = jnp.stack([s for _, s in ret])
        sgla_sc[...] = jnp.stack([s for _, s in gla])
        return carry

    lax.fori_loop(0, n_sub, chunk, 0)

    @pl.when(t == nt - 1)
    def _():
        sr_out_ref[0] = sret_sc[...]
        sg_out_ref[0] = sgla_sc[...]


def _ab_mixer(z, la, cos, sin, rg, gg, *, n_seq, seq_len, chunk, rows_per_step, s_ret=None, s_gla=None):
    M, ZW = z.shape
    R = min(rows_per_step, seq_len)
    nt = seq_len // R
    has_state = s_ret is not None
    in_specs = [pl.BlockSpec((R, ZW), lambda b, t: (b * nt + t, 0)),
                pl.BlockSpec((R, la.shape[1]), lambda b, t: (b * nt + t, 0)),
                pl.BlockSpec((R, LANES), lambda b, t: (t, 0)),
                pl.BlockSpec((R, LANES), lambda b, t: (t, 0)),
                pl.BlockSpec((1, 256), lambda b, t: (0, 0)),
                pl.BlockSpec((1, 256), lambda b, t: (0, 0))]
    args = [z, la, cos, sin, rg, gg]
    sr_spec = pl.BlockSpec((1, H_RET, 256, 256), lambda b, t: (b, 0, 0, 0))
    sg_spec = pl.BlockSpec((1, H_GLA, 128, 256), lambda b, t: (b, 0, 0, 0))
    if has_state:
        in_specs += [sr_spec, sg_spec]
        args += [s_ret, s_gla]
    kern = functools.partial(_ab_kernel, C=chunk, n_sub=R // chunk, has_state=has_state)
    return pl.pallas_call(
        kern, grid=(n_seq, nt), in_specs=in_specs,
        out_specs=[pl.BlockSpec((R, 2048), lambda b, t: (b * nt + t, 0)), sr_spec, sg_spec],
        out_shape=[jax.ShapeDtypeStruct((M, 2048), BF16),
                   jax.ShapeDtypeStruct((n_seq, H_RET, 256, 256), F32),
                   jax.ShapeDtypeStruct((n_seq, H_GLA, 128, 256), F32)],
        scratch_shapes=[pltpu.VMEM((H_RET, 256, 256), F32), pltpu.VMEM((H_GLA, 128, 256), F32)],
        compiler_params=_cparams(("parallel", "arbitrary")), name="ab_mixer")(*args)


def _bias_kernel(rb_ref, o_ref):
    NB = rb_ref.shape[1]
    u = lax.broadcasted_iota(jnp.int32, (NB, TOEP_W), 1)
    d = jnp.where(u < TOEP_W - Q_TILE, u, u - TOEP_W)
    idx = jnp.clip(LEFT_CHUNKS * CHUNK - d, -MAX_REL, MAX_REL) + MAX_REL
    mrow = lax.broadcasted_iota(jnp.int32, (NB, TOEP_W), 0)
    onehot = jnp.where(mrow == idx, 1.0, 0.0).astype(BF16)
    rb = rb_ref[...]
    hi = rb.astype(BF16)
    r1 = rb - hi.astype(F32)
    mid = r1.astype(BF16)
    lo = (r1 - mid.astype(F32)).astype(BF16)
    grow = _dot(hi, onehot) + _dot(mid, onehot) + _dot(lo, onehot)
    i = lax.broadcasted_iota(jnp.int32, (Q_TILE, K_WIN), 0)
    j = lax.broadcasted_iota(jnp.int32, (Q_TILE, K_WIN), 1)
    rel_chunk = j // CHUNK - i // CHUNK
    in_band = (rel_chunk >= 0) & (rel_chunk <= LEFT_CHUNKS)
    for h in range(rb_ref.shape[0]):
        g = jnp.broadcast_to(grow[h:h + 1, :], (Q_TILE, TOEP_W))
        toep = pltpu.roll(g, 0, 1, stride=1, stride_axis=0)[:, :K_WIN]
        o_ref[h] = jnp.where(in_band, toep, NEG_INF)


def _bias_tiles(rel_bias):
    H, NR = rel_bias.shape
    NB = 5 * LANES
    rbp = jnp.pad(rel_bias, ((0, 0), (0, NB - NR)))
    return pl.pallas_call(
        _bias_kernel, out_shape=jax.ShapeDtypeStruct((H, Q_TILE, K_WIN), F32),
        compiler_params=pltpu.CompilerParams(vmem_limit_bytes=VMEM_LIMIT), name="rel_bias_tiles")(rbp)


def _attn_prompt_kernel(q_ref, k_ref, v_ref, bias_ref, o_ref, ko_ref, vo_ref, kpad_sc, vpad_sc):
    T, DH = k_ref.shape
    KEEP = ko_ref.shape[0] // H_ATT
    PAD = LEFT_CHUNKS * CHUNK
    h = pl.program_id(1)
    k = k_ref[...]
    v = v_ref[...]
    ko_ref[pl.ds(h, KEEP, stride=H_ATT), :] = k[T - KEEP:, :]
    vo_ref[pl.ds(h, KEEP, stride=H_ATT), :] = v[T - KEEP:, :]
    kpad_sc[0:PAD, :] = jnp.zeros((PAD, DH), BF16)
    vpad_sc[0:PAD, :] = jnp.zeros((PAD, DH), BF16)
    kpad_sc[PAD:PAD + T, :] = k.astype(BF16)
    vpad_sc[PAD:PAD + T, :] = v.astype(BF16)
    scale = DH ** -0.5
    colj = lax.broadcasted_iota(jnp.int32, (ATT_STRIP, K_WIN), 1)

    def scores(r0):
        q = (q_ref[pl.ds(r0, Q_TILE), :] * scale).astype(BF16)
        return _dot_nt(q, kpad_sc[pl.ds(r0, K_WIN), :])

    def attend(r0, s, masked):
        ps, dens = [], []
        for r in range(0, Q_TILE, ATT_STRIP):
            sr = s[r:r + ATT_STRIP, :] + bias_ref[0, r:r + ATT_STRIP, :]
            if masked:
                sr = jnp.where(colj + r0 >= PAD, sr, NEG_INF)
            mx = jnp.max(sr, axis=-1, keepdims=True)
            p = jnp.exp(sr - mx)
            dens.append(jnp.sum(p, axis=-1, keepdims=True))
            ps.append(p.astype(BF16))
        o = _dot(jnp.concatenate(ps, axis=0), vpad_sc[pl.ds(r0, K_WIN), :])
        o = [o[i * ATT_STRIP:(i + 1) * ATT_STRIP, :] / den for i, den in enumerate(dens)]
        return jnp.concatenate(o, axis=0).astype(o_ref.dtype)

    n_tiles = T // Q_TILE
    n_masked = min(PAD // Q_TILE, n_tiles)
    G = ATT_GROUP if (n_masked % ATT_GROUP == 0 and n_tiles % ATT_GROUP == 0) else 1

    def group(g, carry, *, masked):
        g0 = pl.multiple_of(g * (G * Q_TILE), G * Q_TILE)
        s_next = scores(g0)
        outs = []
        for t in range(G):
            s_cur = s_next
            if t + 1 < G:
                s_next = scores(g0 + (t + 1) * Q_TILE)
            outs.append(attend(g0 + t * Q_TILE, s_cur, masked))
        o_ref[pl.ds(g0, G * Q_TILE), :] = jnp.concatenate(outs, axis=0)
        return carry

    lax.fori_loop(0, n_masked // G, functools.partial(group, masked=True), 0)
    lax.fori_loop(n_masked // G, n_tiles // G, functools.partial(group, masked=False), 0)


def _attn_prompt(z, bias, *, n_seq, seq_len):
    M = z.shape[0]
    DH = 128
    T = seq_len
    keep = min(LEFT_CHUNKS * CHUNK, T)
    kv_spec = pl.BlockSpec((None, None, keep * H_ATT, DH), lambda b, h: (0, b, 0, 0))
    kv_shape = jax.ShapeDtypeStruct((1, n_seq, keep * H_ATT, DH), F32)
    return pl.pallas_call(
        _attn_prompt_kernel, grid=(n_seq, H_ATT),
        in_specs=[pl.BlockSpec((T, DH), lambda b, h: (b, h)),
                  pl.BlockSpec((T, DH), lambda b, h: (b, H_ATT + h)),
                  pl.BlockSpec((T, DH), lambda b, h: (b, 2 * H_ATT + h)),
                  pl.BlockSpec((1, Q_TILE, K_WIN), lambda b, h: (h, 0, 0))],
        out_specs=[pl.BlockSpec((T, DH), lambda b, h: (b, h)), kv_spec, kv_spec],
        out_shape=[jax.ShapeDtypeStruct((M, H_ATT * DH), BF16), kv_shape, kv_shape],
        scratch_shapes=[pltpu.VMEM((LEFT_CHUNKS * CHUNK + T, DH), BF16),
                        pltpu.VMEM((LEFT_CHUNKS * CHUNK + T, DH), BF16)],
        compiler_params=_cparams(("parallel", "arbitrary")), name="attn_prompt")(z, z, z, bias)


def _attn_sample_kernel(q_ref, k_ref, v_ref, kc_ref, vc_ref, bias_ref, o_ref, ko_ref, vo_ref):
    T = q_ref.shape[0]
    W = kc_ref.shape[0] // H_ATT
    DH = 128
    scale = DH ** -0.5
    outs = []
    for h in range(H_ATT):
        cs = pl.ds(h * DH, DH)
        q = (q_ref[:, cs] * scale).astype(BF16)
        kn = k_ref[:, cs]
        vn = v_ref[:, cs]
        ko_ref[pl.ds(h, T, stride=H_ATT), :] = kn
        vo_ref[pl.ds(h, T, stride=H_ATT), :] = vn
        kn = kn.astype(BF16)
        vn = vn.astype(BF16)
        kc = kc_ref[pl.ds(h, W, stride=H_ATT), :].astype(BF16)
        vc = vc_ref[pl.ds(h, W, stride=H_ATT), :].astype(BF16)
        s1 = _dot_nt(q, kc) + bias_ref[h, :, 0:W]
        s2 = _dot_nt(q, kn) + bias_ref[h, :, W:W + T]
        mx = jnp.maximum(jnp.max(s1, axis=-1, keepdims=True), jnp.max(s2, axis=-1, keepdims=True))
        p1 = jnp.exp(s1 - mx)
        p2 = jnp.exp(s2 - mx)
        den = jnp.sum(p1, axis=-1, keepdims=True) + jnp.sum(p2, axis=-1, keepdims=True)
        o = (_dot(p1.astype(BF16), vc) + _dot(p2.astype(BF16), vn)) / den
        outs.append(o.astype(o_ref.dtype))
    o_ref[...] = jnp.concatenate(outs, axis=-1)


def _attn_sample(z, kc, vc, bias, *, n_seq, seq_len):
    M = z.shape[0]
    T = seq_len
    DH = 128
    E = H_ATT * DH
    WH = kc.shape[2]
    cache_spec = pl.BlockSpec((None, None, WH, DH), lambda b: (0, b, 0, 0))
    new_spec = pl.BlockSpec((None, None, T * H_ATT, DH), lambda b: (0, b, 0, 0))
    new_shape = jax.ShapeDtypeStruct((1, n_seq, T * H_ATT, DH), F32)
    return pl.pallas_call(
        _attn_sample_kernel, grid=(n_seq,),
        in_specs=[pl.BlockSpec((T, E), lambda b: (b, 0)),
                  pl.BlockSpec((T, E), lambda b: (b, 1)),
                  pl.BlockSpec((T, E), lambda b: (b, 2)),
                  cache_spec, cache_spec,
                  pl.BlockSpec((H_ATT, T, K_WIN), lambda b: (0, 0, 0))],
        out_specs=[pl.BlockSpec((T, E), lambda b: (b, 0)), new_spec, new_spec],
        out_shape=[jax.ShapeDtypeStruct((M, E), BF16), new_shape, new_shape],
        compiler_params=_cparams(("parallel",)), name="attn_sample")(z, z, z, kc, vc, bias)


def _rope_tables(pos):
    half = LANES
    inv = ROPE_BASE ** (-jnp.arange(half, dtype=F32) / half)
    ang = pos.astype(F32)[:, None] * inv[None, :]
    return jnp.cos(ang), jnp.sin(ang)


def _trunk(x, pos, wts, bias, *, n_seq, seq_len, state=None):
    D = x.shape[-1]
    M = n_seq * seq_len
    x = x.reshape(M, D)
    sample = state is not None
    chunk = min(CHUNK, seq_len)
    cos, sin = _rope_tables(pos)

    n_main = wts['w_in'].shape[1] - GLA_GATE_RANK
    z, la = _norm_matmul(x, wts['norm_mix_g'][0], wts['w_in'], bm=MM_BM, bn=MM_BN, n_cols=n_main,
                         gate=(wts['w_in_lo'], wts['gate_w2'], wts['gate_b']))
    if sample:
        o, s_ret, s_gla = _ab_mixer(z, la, cos, sin, wts['ret_g'], wts['gla_g'], n_seq=n_seq, seq_len=seq_len,
                                    chunk=chunk, rows_per_step=256, s_ret=state['ret'], s_gla=state['gla'])
    else:
        o, s_ret, s_gla = _ab_mixer(z, la, cos, sin, wts['ret_g'], wts['gla_g'], n_seq=n_seq, seq_len=seq_len,
                                    chunk=chunk, rows_per_step=256)
    x = _proj_residual(x, o, wts['w_out_ab'], bm=PROJ_BM)
    hist0 = state['conv'][0] if sample else None
    x, tail0 = _conv_ffn(x, wts['norm_ffn_g'][0], wts['w_up'], wts['conv_w'][0], wts['conv_b'][0],
                         wts['w_down'], layer=0, bm=FFN_BM, bf=FFN_BF, seq_len=seq_len, hist=hist0)

    z = _norm_matmul(x, wts['norm_mix_g'][1], wts['w_qkv'], bm=MM_BM, bn=MM_BN)
    if sample:
        o, new_k, new_v = _attn_sample(z, state['k'], state['v'], bias, n_seq=n_seq, seq_len=seq_len)
    else:
        o, new_k, new_v = _attn_prompt(z, bias, n_seq=n_seq, seq_len=seq_len)
    x = _proj_residual(x, o, wts['w_out_att'], bm=PROJ_BM)
    hist1 = state['conv'][1] if sample else None
    y, tail1 = _conv_ffn(x, wts['norm_ffn_g'][1], wts['w_up'], wts['conv_w'][1], wts['conv_b'][1],
                         wts['w_down'], layer=1, bm=FFN_BM, bf=FFN_BF, seq_len=seq_len, hist=hist1,
                         final_g=wts['norm_final_g'])

    def conv_state(tail):
        F = tail.shape[-1]
        t = tail.reshape(n_seq, -1, SUBLANES, F)[:, -1]
        return t[:, SUBLANES - (CONV_W - 1):]

    conv = jnp.stack([conv_state(tail0), conv_state(tail1)])
    kv_shape = (1, n_seq, -1, H_ATT, 128)
    return (y.reshape(n_seq, seq_len, D), s_ret[None], s_gla[None], new_k.reshape(kv_shape), new_v.reshape(kv_shape),
            conv)


def kernel(x_prompt, x_sample, state_ret, state_gla, cache_attn_k, cache_attn_v, state_ffn_conv, norm_mix_g, w_in_ab, gla_gate_w2, gla_gate_b, ret_norm_g, gla_norm_g, w_out_ab, w_qkv_att, rel_bias_att, w_out_att, norm_ffn_g, w_ffn_up, ffn_conv_w, ffn_conv_b, w_ffn_down, norm_final_g):
    B, T, D = x_prompt.shape
    SB, ST, _ = x_sample.shape
    main = w_in_ab.shape[2] - GLA_GATE_RANK
    wts = dict(
        norm_mix_g=norm_mix_g[:, None, :],
        w_in=w_in_ab[0].astype(BF16),
        w_in_lo=jnp.pad(w_in_ab[0, :, main:], ((0, 0), (0, LANES - GLA_GATE_RANK))).astype(BF16),
        gate_w2=jnp.pad(gla_gate_w2[0], ((0, LANES - GLA_GATE_RANK), (0, 0))).astype(BF16),
        gate_b=gla_gate_b[0][None, :],
        ret_g=ret_norm_g[0][None, :],
        gla_g=gla_norm_g[0][None, :],
        w_out_ab=w_out_ab[0].astype(BF16),
        w_qkv=w_qkv_att[0].astype(BF16),
        w_out_att=w_out_att[0].astype(BF16),
        norm_ffn_g=norm_ffn_g[:, None, :],
        w_up=w_ffn_up.astype(BF16),
        conv_w=ffn_conv_w,
        conv_b=ffn_conv_b[:, None, :],
        w_down=w_ffn_down.astype(BF16),
        norm_final_g=norm_final_g[None, :],
    )
    bias = _bias_tiles(rel_bias_att[0])
    state = dict(
        ret=state_ret[0], gla=state_gla[0],
        k=cache_attn_k.reshape(cache_attn_k.shape[:2] + (-1, cache_attn_k.shape[-1])),
        v=cache_attn_v.reshape(cache_attn_v.shape[:2] + (-1, cache_attn_v.shape[-1])),
        conv=jnp.pad(state_ffn_conv, ((0, 0), (0, 0), (SUBLANES - (CONV_W - 1), 0), (0, 0))),
    )
    outs_p = _trunk(x_prompt, jnp.arange(T), wts, bias, n_seq=B, seq_len=T)
    outs_s = _trunk(x_sample, PAST_LEN + jnp.arange(ST), wts, bias, n_seq=SB, seq_len=ST, state=state)
    return (outs_p[0], outs_s[0]) + outs_p[1:] + outs_s[1:]
```

```python
import functools
import math

import jax
import jax.numpy as jnp
import numpy as np
from jax import lax
from jax.experimental import pallas as pl
from jax.experimental.pallas import tpu as pltpu

F32 = jnp.float32
BF16 = jnp.bfloat16

CHUNK = 64
RMS_EPS = 1e-6
H_RET = 4
H_GLA = 4
GLA_GATE_RANK = 16
GLA_GATE_TAU = 16.0
ROPE_BASE = 10000.0
H_ATT = 16
LEFT_CHUNKS = 8
MAX_REL = 256
NEG_INF = -1e30
CONV_W = 3
PAST_LEN = 2048

LANES = 128
SUBLANES = 8
VMEM_LIMIT = 60 * 1024 * 1024

BAND = (LEFT_CHUNKS + 1) * CHUNK
Q_TILE = 2 * CHUNK
K_WIN = BAND + CHUNK
TOEP_W = 1024

MM_BM = 1024
IN_BN, QKV_BN = 1792, 2048
PROJ_BM = 1024
FFN_BM, FFN_BF = 1024, 512
AB_ROWS = 512
FFN_SPLIT = 2
ATT_GROUP = 12
ATT_STRIP = 32


def _cparams(sem):
    return pltpu.CompilerParams(dimension_semantics=sem, vmem_limit_bytes=VMEM_LIMIT)


def _rms(x, g):
    return x * lax.rsqrt(jnp.mean(x * x, axis=-1, keepdims=True) + RMS_EPS) * g


def _silu(x):
    return x * jax.nn.sigmoid(x)


def _dot(a, b):
    return jnp.dot(a, b, preferred_element_type=F32)


def _dot_nt(a, b):
    return lax.dot_general(a, b, (((1,), (1,)), ((), ())), preferred_element_type=F32)


def _dot_tn(a, b):
    return lax.dot_general(a, b, (((0,), (0,)), ((), ())), preferred_element_type=F32)


def _norm_mm_kernel(x_ref, g_ref, w_ref, o_ref, h_sc):
    @pl.when(pl.program_id(1) == 0)
    def _():
        h_sc[...] = _rms(x_ref[...], g_ref[...]).astype(BF16)

    o_ref[...] = _dot(h_sc[...], w_ref[...])


def _norm_mm_gate_kernel(x_ref, g_ref, w_ref, wlo_ref, w2_ref, gb_ref, o_ref, la_ref, h_sc):
    @pl.when(pl.program_id(1) == 0)
    def _():
        h = _rms(x_ref[...], g_ref[...]).astype(BF16)
        h_sc[...] = h
        lo = _dot(h, wlo_ref[...])
        xg = _dot(lo.astype(BF16), w2_ref[...]) + gb_ref[...]
        la_ref[...] = jax.nn.log_sigmoid(xg) / GLA_GATE_TAU

    o_ref[...] = _dot(h_sc[...], w_ref[...])


def _norm_matmul(x, g, w, *, bm, bn, gate=None, n_cols=None):
    M, D = x.shape
    N = w.shape[1] if n_cols is None else n_cols
    bm = min(bm, M)
    grid = (M // bm, N // bn)
    x_spec = pl.BlockSpec((bm, D), lambda i, j: (i, 0))
    g_spec = pl.BlockSpec((1, D), lambda i, j: (0, 0))
    w_spec = pl.BlockSpec((D, bn), lambda i, j: (0, j))
    o_spec = pl.BlockSpec((bm, bn), lambda i, j: (i, j))
    scratch = [pltpu.VMEM((bm, D), BF16)]
    if gate is None:
        return pl.pallas_call(
            _norm_mm_kernel, grid=grid, in_specs=[x_spec, g_spec, w_spec], out_specs=o_spec,
            out_shape=jax.ShapeDtypeStruct((M, N), F32), scratch_shapes=scratch,
            compiler_params=_cparams(("parallel", "arbitrary")), name="norm_matmul")(x, g, w)
    wlo, w2, gb = gate
    G = w2.shape[1]
    return pl.pallas_call(
        _norm_mm_gate_kernel, grid=grid,
        in_specs=[x_spec, g_spec, w_spec,
                  pl.BlockSpec(wlo.shape, lambda i, j: (0, 0)),
                  pl.BlockSpec(w2.shape, lambda i, j: (0, 0)),
                  pl.BlockSpec((1, G), lambda i, j: (0, 0))],
        out_specs=[o_spec, pl.BlockSpec((bm, G), lambda i, j: (i, 0))],
        out_shape=[jax.ShapeDtypeStruct((M, N), F32), jax.ShapeDtypeStruct((M, G), F32)],
        scratch_shapes=scratch,
        compiler_params=_cparams(("parallel", "arbitrary")), name="norm_matmul_gate")(x, g, w, wlo, w2, gb)


def _proj_res_kernel(x_ref, o_ref, w_ref, y_ref):
    y_ref[...] = x_ref[...] + _dot(o_ref[...], w_ref[...])


def _proj_residual(x, o, w, *, bm):
    M, D = x.shape
    K = o.shape[1]
    bm = min(bm, M)
    return pl.pallas_call(
        _proj_res_kernel, grid=(M // bm,),
        in_specs=[pl.BlockSpec((bm, D), lambda i: (i, 0)),
                  pl.BlockSpec((bm, K), lambda i: (i, 0)),
                  pl.BlockSpec((K, D), lambda i: (0, 0))],
        out_specs=pl.BlockSpec((bm, D), lambda i: (i, 0)),
        out_shape=jax.ShapeDtypeStruct((M, D), F32),
        compiler_params=_cparams(("parallel",)), name="proj_residual")(x, o, w)


def _ffn_kernel(*refs, nseq, seq_rows, tiles_per_seq, final_norm):
    refs = list(refs)
    x_ref, g_ref, wg_ref, wu_ref, cw_ref, cb_ref, wd_ref = refs[:7]
    pos = 7
    hist_ref = None
    if tiles_per_seq is None:
        hist_ref = refs[pos]
        pos += 1
    gf_ref = None
    if final_norm:
        gf_ref = refs[pos]
        pos += 1
    y_ref, tail_ref = refs[pos:pos + 2]
    h_sc, gbuf_sc = refs[pos + 2:pos + 4]
    carry_sc = refs[pos + 4] if tiles_per_seq is not None else None

    m = pl.program_id(0)
    f = pl.program_id(1)
    nf = pl.num_programs(1)
    L = seq_rows
    H = SUBLANES

    @pl.when(f == 0)
    def _():
        x = x_ref[...]
        h_sc[...] = _rms(x, g_ref[...]).astype(BF16)
        y_ref[...] = x
        if carry_sc is not None:
            @pl.when(m == 0)
            def _():
                carry_sc[...] = jnp.zeros(carry_sc.shape, F32)

    h = h_sc[...]
    bm = h.shape[0]
    bf = wg_ref.shape[1]
    hw = bf // FFN_SPLIT
    gates = [_dot(h, wg_ref[:, c * hw:(c + 1) * hw]) for c in range(FFN_SPLIT)]
    ups = [_dot(h, wu_ref[:, c * hw:(c + 1) * hw]) for c in range(FFN_SPLIT)]
    if tiles_per_seq is None:
        hist = hist_ref[...]
    else:
        hist = jnp.where((m % tiles_per_seq) == 0, 0.0, carry_sc[f])
    acc = None
    tails = []
    for c in range(FFN_SPLIT):
        cs = slice(c * hw, (c + 1) * hw)
        g3 = gates[c].reshape(nseq, L, hw)
        gbuf_sc[:, H:H + L, cs] = g3
        gbuf_sc[:, 0:H, cs] = hist[:, :, cs]
        tails.append(g3[:, L - H:L, :])
        g1 = gbuf_sc[:, H - 1:H - 1 + L, cs]
        g2 = gbuf_sc[:, H - 2:H - 2 + L, cs]
        gc = cb_ref[:, cs] + g2 * cw_ref[0:1, cs]
        gc = gc + g1 * cw_ref[1:2, cs]
        gc = gc + g3 * cw_ref[2:3, cs]
        act = (_silu(gc) * ups[c].reshape(nseq, L, hw)).reshape(bm, hw).astype(BF16)
        d = _dot(act, wd_ref[cs, :])
        acc = d if acc is None else acc + d
    tail = jnp.concatenate(tails, axis=-1)
    tail_ref[...] = tail
    if carry_sc is not None:
        carry_sc[f] = tail
    y_ref[...] += acc

    if final_norm:
        @pl.when(f == nf - 1)
        def _():
            y_ref[...] = _rms(y_ref[...], gf_ref[...])


def _conv_ffn(x, g, w_up, cw, cb, wd, *, layer, bm, bf, seq_len, hist=None, final_g=None):
    M, D = x.shape
    F = wd.shape[1]
    bm = min(bm, M)
    nf = F // bf
    if seq_len >= bm:
        nseq, seq_rows, tiles_per_seq = 1, bm, seq_len // bm
        n_groups = M // bm
    else:
        nseq, seq_rows, tiles_per_seq = bm // seq_len, seq_len, None
        n_groups = M // seq_len
    in_specs = [pl.BlockSpec((bm, D), lambda i, j: (i, 0)),
                pl.BlockSpec((1, D), lambda i, j: (0, 0)),
                pl.BlockSpec((None, D, bf), lambda i, j: (layer, 0, j)),
                pl.BlockSpec((None, D, bf), lambda i, j: (layer, 0, nf + j)),
                pl.BlockSpec((CONV_W, bf), lambda i, j: (0, j)),
                pl.BlockSpec((1, bf), lambda i, j: (0, j)),
                pl.BlockSpec((None, bf, D), lambda i, j: (layer, j, 0))]
    args = [x, g, w_up, w_up, cw, cb, wd]
    if tiles_per_seq is None:
        in_specs.append(pl.BlockSpec((nseq, SUBLANES, bf), lambda i, j: (i, 0, j)))
        args.append(hist)
    if final_g is not None:
        in_specs.append(pl.BlockSpec((1, D), lambda i, j: (0, 0)))
        args.append(final_g)
    scratch = [pltpu.VMEM((bm, D), BF16), pltpu.VMEM((nseq, SUBLANES + seq_rows, bf), F32)]
    if tiles_per_seq is not None:
        scratch.append(pltpu.VMEM((nf, 1, SUBLANES, bf), F32))
    kern = functools.partial(_ffn_kernel, nseq=nseq, seq_rows=seq_rows, tiles_per_seq=tiles_per_seq,
                             final_norm=final_g is not None)
    return pl.pallas_call(
        kern, grid=(M // bm, nf), in_specs=in_specs,
        out_specs=[pl.BlockSpec((bm, D), lambda i, j: (i, 0)),
                   pl.BlockSpec((nseq, SUBLANES, bf), lambda i, j: (i, 0, j))],
        out_shape=[jax.ShapeDtypeStruct((M, D), F32), jax.ShapeDtypeStruct((n_groups, SUBLANES, F), F32)],
        scratch_shapes=scratch,
        compiler_params=_cparams(("arbitrary", "arbitrary")), name="conv_ffn")(*args)


def _ab_kernel(*refs, C, n_sub, has_state):
    refs = list(refs)
    z_ref, la_ref, cos_ref, sin_ref, rg_ref, gg_ref = refs[:6]
    pos = 6
    if has_state:
        sr0_ref, sg0_ref = refs[pos:pos + 2]
        pos += 2
    o_ref, sr_out_ref, sg_out_ref = refs[pos:pos + 3]
    sret_sc, sgla_sc = refs[pos + 3:pos + 5]

    t = pl.program_id(1)
    nt = pl.num_programs(1)
    DK_R = 256
    DV = 256
    DK_G = 128

    @pl.when(t == 0)
    def _():
        if has_state:
            sret_sc[...] = sr0_ref[0]
            sgla_sc[...] = sg0_ref[0]
        else:
            sret_sc[...] = jnp.zeros(sret_sc.shape, F32)
            sgla_sc[...] = jnp.zeros(sgla_sc.shape, F32)

    row = lax.broadcasted_iota(jnp.int32, (C, C), 0)
    col = lax.broadcasted_iota(jnp.int32, (C, C), 1)
    causal = row >= col
    diff = jnp.where(causal, row - col, 0).astype(F32)
    tri = jnp.where(causal, 1.0, 0.0).astype(BF16)
    ridx = lax.broadcasted_iota(jnp.int32, (C, 1), 0).astype(F32)
    SB = SUBLANES
    nsb = C // SB
    lane_c = lax.broadcasted_iota(jnp.int32, (C, LANES), 1)
    row_c = lax.broadcasted_iota(jnp.int32, (C, LANES), 0)
    blk0_c = (row_c // SB) * SB
    ones = jnp.ones((LANES, LANES), BF16)
    rg = rg_ref[...]
    gg = gg_ref[...]

    def chunk(c, carry):
        r0 = pl.multiple_of(c * C, C)
        rows = pl.ds(r0, C)
        cos = cos_ref[rows, :]
        sin = sin_ref[rows, :]

        def rope(x):
            x1 = x[:, :LANES]
            x2 = x[:, LANES:]
            return jnp.concatenate([x1 * cos - x2 * sin, x1 * sin + x2 * cos], axis=-1)

        def ret_head(h):
            lg = math.log1p(-2.0 ** (-5.0 - h))
            q = rope(z_ref[rows, pl.ds(h * DK_R, DK_R)])
            k = rope(z_ref[rows, pl.ds(H_RET * DK_R + h * DK_R, DK_R)]) * (DK_R ** -0.5)
            v = z_ref[rows, pl.ds(2 * H_RET * DK_R + h * DV, DV)].astype(BF16)
            gate = z_ref[rows, pl.ds(2 * H_RET * DK_R + H_RET * DV + h * DV, DV)]
            S = sret_sc[h]
            decay = jnp.where(causal, jnp.exp(diff * lg), 0.0)
            scores = _dot_nt(q.astype(BF16), k.astype(BF16)) * decay
            o = _dot(scores.astype(BF16), v)
            q_dec = jnp.exp((ridx + 1.0) * lg)
            o = o + _dot((q * q_dec).astype(BF16), S.astype(BF16))
            k_dec = jnp.exp((C - 1.0 - ridx) * lg)
            s_new = math.exp(C * lg) * S + _dot_tn((k * k_dec).astype(BF16), v)
            return (_rms(o, rg) * _silu(gate)).astype(o_ref.dtype), s_new

        base = 2 * H_RET * DK_R + 2 * H_RET * DV

        def gla_head(h):
            q = z_ref[rows, pl.ds(base + h * DK_G, DK_G)] * (DK_G ** -0.5)
            k = z_ref[rows, pl.ds(base + H_GLA * DK_G + h * DK_G, DK_G)]
            v = z_ref[rows, pl.ds(base + 2 * H_GLA * DK_G + h * DV, DV)].astype(BF16)
            gate = z_ref[rows, pl.ds(base + 2 * H_GLA * DK_G + H_GLA * DV + h * DV, DV)]
            la = la_ref[rows, pl.ds(h * DK_G, DK_G)]
            la_hi = la.astype(BF16)
            la_lo = (la - la_hi.astype(F32)).astype(BF16)
            b2 = _dot(tri, jnp.concatenate([la_hi, la_lo], axis=-1))
            b = b2[:, :DK_G] + b2[:, DK_G:]
            S = sgla_sc[h]
            o = _dot((q * jnp.exp(b)).astype(BF16), S.astype(BF16))
            b_last = b[C - 1:C, :]
            kd = k * jnp.exp(b_last - b)
            e_col = jnp.transpose(jnp.broadcast_to(jnp.exp(b_last), (LANES, DK_G)))[:, 0:1]
            s_new = e_col * S + _dot_tn(kd.astype(BF16), v)
            q3 = q.reshape(nsb, SB, DK_G)
            k3 = k.reshape(nsb, SB, DK_G)
            b3 = b.reshape(nsb, SB, DK_G)
            pieces = []
            for j in range(SB):
                w = jnp.exp(jnp.minimum(b3 - b3[:, j:j + 1, :], 0.0))
                pieces.append((q3 * k3[:, j:j + 1, :] * w).reshape(C, DK_G))
            sums = _dot(jnp.concatenate(pieces, axis=0).astype(BF16), ones)
            diag = jnp.zeros((C, LANES), F32)
            for j in range(SB):
                diag = jnp.where(lane_c == blk0_c + j, sums[j * C:(j + 1) * C, :], diag)
            diag = jnp.where(lane_c <= row_c, diag, 0.0)
            blocks = [diag[0:SB, :C]]
            for I in range(1, nsb):
                i0 = I * SB
                b_r = b[i0:i0 + 1, :]
                qt = q[i0:i0 + SB, :] * jnp.exp(b[i0:i0 + SB, :] - b_r)
                before = row_c < i0
                kt = jnp.where(before, k * jnp.exp(jnp.where(before, b_r - b, 0.0)), 0.0)
                blocks.append(_dot_nt(qt.astype(BF16), kt.astype(BF16)) + diag[i0:i0 + SB, :C])
            scores = jnp.concatenate(blocks, axis=0)
            o = o + _dot(scores.astype(BF16), v)
            return (_rms(o, gg) * _silu(gate)).astype(o_ref.dtype), s_new

        assert H_RET == H_GLA
        ret, gla = [], []
        for h in range(H_RET):
            ret.append(ret_head(h))
            gla.append(gla_head(h))
        o_ref[rows, :] = jnp.concatenate([o for o, _ in ret] + [o for o, _ in gla], axis=-1)
        sret_sc[...] = jnp.stack([s for _, s in ret])
        sgla_sc[...] = jnp.stack([s for _, s in gla])
        return carry

    lax.fori_loop(0, n_sub, chunk, 0)

    @pl.when(t == nt - 1)
    def _():
        sr_out_ref[0] = sret_sc[...]
        sg_out_ref[0] = sgla_sc[...]


def _ab_mixer(z, la, cos, sin, rg, gg, *, n_seq, seq_len, chunk, rows_per_step, s_ret=None, s_gla=None):
    M, ZW = z.shape
    R = min(rows_per_step, seq_len)
    nt = seq_len // R
    has_state = s_ret is not None
    in_specs = [pl.BlockSpec((R, ZW), lambda b, t: (b * nt + t, 0)),
                pl.BlockSpec((R, la.shape[1]), lambda b, t: (b * nt + t, 0)),
                pl.BlockSpec((R, LANES), lambda b, t: (t, 0)),
                pl.BlockSpec((R, LANES), lambda b, t: (t, 0)),
                pl.BlockSpec((1, 256), lambda b, t: (0, 0)),
                pl.BlockSpec((1, 256), lambda b, t: (0, 0))]
    args = [z, la, cos, sin, rg, gg]
    sr_spec = pl.BlockSpec((1, H_RET, 256, 256), lambda b, t: (b, 0, 0, 0))
    sg_spec = pl.BlockSpec((1, H_GLA, 128, 256), lambda b, t: (b, 0, 0, 0))
    if has_state:
        in_specs += [sr_spec, sg_spec]
        args += [s_ret, s_gla]
    kern = functools.partial(_ab_kernel, C=chunk, n_sub=R // chunk, has_state=has_state)
    return pl.pallas_call(
        kern, grid=(n_seq, nt), in_specs=in_specs,
        out_specs=[pl.BlockSpec((R, 2048), lambda b, t: (b * nt + t, 0)), sr_spec, sg_spec],
        out_shape=[jax.ShapeDtypeStruct((M, 2048), BF16),
                   jax.ShapeDtypeStruct((n_seq, H_RET, 256, 256), F32),
                   jax.ShapeDtypeStruct((n_seq, H_GLA, 128, 256), F32)],
        scratch_shapes=[pltpu.VMEM((H_RET, 256, 256), F32), pltpu.VMEM((H_GLA, 128, 256), F32)],
        compiler_params=_cparams(("parallel", "arbitrary")), name="ab_mixer")(*args)


def _bias_kernel(rb_ref, o_ref):
    NB = rb_ref.shape[1]
    u = lax.broadcasted_iota(jnp.int32, (NB, TOEP_W), 1)
    d = jnp.where(u < TOEP_W - Q_TILE, u, u - TOEP_W)
    idx = jnp.clip(LEFT_CHUNKS * CHUNK - d, -MAX_REL, MAX_REL) + MAX_REL
    mrow = lax.broadcasted_iota(jnp.int32, (NB, TOEP_W), 0)
    onehot = jnp.where(mrow == idx, 1.0, 0.0).astype(BF16)
    rb = rb_ref[...]
    hi = rb.astype(BF16)
    r1 = rb - hi.astype(F32)
    mid = r1.astype(BF16)
    lo = (r1 - mid.astype(F32)).astype(BF16)
    grow = _dot(hi, onehot) + _dot(mid, onehot) + _dot(lo, onehot)
    i = lax.broadcasted_iota(jnp.int32, (Q_TILE, K_WIN), 0)
    j = lax.broadcasted_iota(jnp.int32, (Q_TILE, K_WIN), 1)
    rel_chunk = j // CHUNK - i // CHUNK
    in_band = (rel_chunk >= 0) & (rel_chunk <= LEFT_CHUNKS)
    for h in range(rb_ref.shape[0]):
        g = jnp.broadcast_to(grow[h:h + 1, :], (Q_TILE, TOEP_W))
        toep = pltpu.roll(g, 0, 1, stride=1, stride_axis=0)[:, :K_WIN]
        o_ref[h] = jnp.where(in_band, toep, NEG_INF)


def _bias_tiles(rel_bias):
    H, NR = rel_bias.shape
    NB = 5 * LANES
    rbp = jnp.pad(rel_bias, ((0, 0), (0, NB - NR)))
    return pl.pallas_call(
        _bias_kernel, out_shape=jax.ShapeDtypeStruct((H, Q_TILE, K_WIN), F32),
        compiler_params=pltpu.CompilerParams(vmem_limit_bytes=VMEM_LIMIT), name="rel_bias_tiles")(rbp)


def _attn_prompt_kernel(q_ref, k_ref, v_ref, bias_ref, o_ref, ko_ref, vo_ref, kpad_sc, vpad_sc):
    T, DH = k_ref.shape
    KEEP = ko_ref.shape[0] // H_ATT
    PAD = LEFT_CHUNKS * CHUNK
    h = pl.program_id(1)
    k = k_ref[...]
    v = v_ref[...]
    ko_ref[pl.ds(h, KEEP, stride=H_ATT), :] = k[T - KEEP:, :]
    vo_ref[pl.ds(h, KEEP, stride=H_ATT), :] = v[T - KEEP:, :]
    kpad_sc[0:PAD, :] = jnp.zeros((PAD, DH), BF16)
    vpad_sc[0:PAD, :] = jnp.zeros((PAD, DH), BF16)
    kpad_sc[PAD:PAD + T, :] = k.astype(BF16)
    vpad_sc[PAD:PAD + T, :] = v.astype(BF16)
    scale = DH ** -0.5
    colj = lax.broadcasted_iota(jnp.int32, (ATT_STRIP, K_WIN), 1)

    def scores(r0):
        q = (q_ref[pl.ds(r0, Q_TILE), :] * scale).astype(BF16)
        return _dot_nt(q, kpad_sc[pl.ds(r0, K_WIN), :])

    def attend(r0, s, masked):
        ps, dens = [], []
        for r in range(0, Q_TILE, ATT_STRIP):
            sr = s[r:r + ATT_STRIP, :] + bias_ref[0, r:r + ATT_STRIP, :]
            if masked:
                sr = jnp.where(colj + r0 >= PAD, sr, NEG_INF)
            mx = jnp.max(sr, axis=-1, keepdims=True)
            p = jnp.exp(sr - mx)
            dens.append(jnp.sum(p, axis=-1, keepdims=True))
            ps.append(p.astype(BF16))
        o = _dot(jnp.concatenate(ps, axis=0), vpad_sc[pl.ds(r0, K_WIN), :])
        o = [o[i * ATT_STRIP:(i + 1) * ATT_STRIP, :] / den for i, den in enumerate(dens)]
        return jnp.concatenate(o, axis=0).astype(o_ref.dtype)

    n_tiles = T // Q_TILE
    n_masked = min(PAD // Q_TILE, n_tiles)
    n_rest = n_tiles - n_masked
    G = ATT_GROUP if n_rest % ATT_GROUP == 0 else 1

    def group(r0, n, n_mask):
        s_next = scores(r0)
        outs = []
        for t in range(n):
            s_cur = s_next
            if t + 1 < n:
                s_next = scores(r0 + (t + 1) * Q_TILE)
            outs.append(attend(r0 + t * Q_TILE, s_cur, t < n_mask))
        o_ref[pl.ds(r0, n * Q_TILE), :] = jnp.concatenate(outs, axis=0)

    if G == n_rest:
        group(0, n_tiles, n_masked)
    else:
        group(0, n_masked, n_masked)

        def rest(g, carry):
            group(pl.multiple_of(n_masked * Q_TILE + g * (G * Q_TILE), Q_TILE), G, 0)
            return carry

        lax.fori_loop(0, n_rest // G, rest, 0)


def _attn_prompt(z, bias, *, n_seq, seq_len):
    M = z.shape[0]
    DH = 128
    T = seq_len
    keep = min(LEFT_CHUNKS * CHUNK, T)
    kv_spec = pl.BlockSpec((None, None, keep * H_ATT, DH), lambda b, h: (0, b, 0, 0))
    kv_shape = jax.ShapeDtypeStruct((1, n_seq, keep * H_ATT, DH), F32)
    return pl.pallas_call(
        _attn_prompt_kernel, grid=(n_seq, H_ATT),
        in_specs=[pl.BlockSpec((T, DH), lambda b, h: (b, h)),
                  pl.BlockSpec((T, DH), lambda b, h: (b, H_ATT + h)),
                  pl.BlockSpec((T, DH), lambda b, h: (b, 2 * H_ATT + h)),
                  pl.BlockSpec((1, Q_TILE, K_WIN), lambda b, h: (h, 0, 0))],
        out_specs=[pl.BlockSpec((T, DH), lambda b, h: (b, h)), kv_spec, kv_spec],
        out_shape=[jax.ShapeDtypeStruct((M, H_ATT * DH), BF16), kv_shape, kv_shape],
        scratch_shapes=[pltpu.VMEM((LEFT_CHUNKS * CHUNK + T, DH), BF16),
                        pltpu.VMEM((LEFT_CHUNKS * CHUNK + T, DH), BF16)],
        compiler_params=_cparams(("parallel", "arbitrary")), name="attn_prompt")(z, z, z, bias)


def _attn_sample_kernel(q_ref, k_ref, v_ref, kc_ref, vc_ref, bias_ref, o_ref, ko_ref, vo_ref):
    T = q_ref.shape[0]
    W = kc_ref.shape[0] // H_ATT
    DH = 128
    scale = DH ** -0.5
    def scores(h):
        cs = pl.ds(h * DH, DH)
        q = (q_ref[:, cs] * scale).astype(BF16)
        kn = k_ref[:, cs]
        ko_ref[pl.ds(h, T, stride=H_ATT), :] = kn
        kc = kc_ref[pl.ds(h, W, stride=H_ATT), :].astype(BF16)
        s1 = _dot_nt(q, kc) + bias_ref[h, :, 0:W]
        s2 = _dot_nt(q, kn.astype(BF16)) + bias_ref[h, :, W:W + T]
        return s1, s2

    def attend(h, s1, s2):
        cs = pl.ds(h * DH, DH)
        vn = v_ref[:, cs]
        vo_ref[pl.ds(h, T, stride=H_ATT), :] = vn
        vc = vc_ref[pl.ds(h, W, stride=H_ATT), :].astype(BF16)
        mx = jnp.maximum(jnp.max(s1, axis=-1, keepdims=True), jnp.max(s2, axis=-1, keepdims=True))
        p1 = jnp.exp(s1 - mx)
        p2 = jnp.exp(s2 - mx)
        den = jnp.sum(p1, axis=-1, keepdims=True) + jnp.sum(p2, axis=-1, keepdims=True)
        o = (_dot(p1.astype(BF16), vc) + _dot(p2.astype(BF16), vn.astype(BF16))) / den
        return o.astype(o_ref.dtype)

    s_next = scores(0)
    outs = []
    for h in range(H_ATT):
        s_cur = s_next
        if h + 1 < H_ATT:
            s_next = scores(h + 1)
        outs.append(attend(h, *s_cur))
    o_ref[...] = jnp.concatenate(outs, axis=-1)


def _attn_sample(z, kc, vc, bias, *, n_seq, seq_len):
    M = z.shape[0]
    T = seq_len
    DH = 128
    E = H_ATT * DH
    WH = kc.shape[2]
    cache_spec = pl.BlockSpec((None, None, WH, DH), lambda b: (0, b, 0, 0))
    new_spec = pl.BlockSpec((None, None, T * H_ATT, DH), lambda b: (0, b, 0, 0))
    new_shape = jax.ShapeDtypeStruct((1, n_seq, T * H_ATT, DH), F32)
    return pl.pallas_call(
        _attn_sample_kernel, grid=(n_seq,),
        in_specs=[pl.BlockSpec((T, E), lambda b: (b, 0)),
                  pl.BlockSpec((T, E), lambda b: (b, 1)),
                  pl.BlockSpec((T, E), lambda b: (b, 2)),
                  cache_spec, cache_spec,
                  pl.BlockSpec((H_ATT, T, K_WIN), lambda b: (0, 0, 0))],
        out_specs=[pl.BlockSpec((T, E), lambda b: (b, 0)), new_spec, new_spec],
        out_shape=[jax.ShapeDtypeStruct((M, E), BF16), new_shape, new_shape],
        compiler_params=_cparams(("parallel",)), name="attn_sample")(z, z, z, kc, vc, bias)


def _rope_tables(pos):
    half = LANES
    inv = ROPE_BASE ** (-jnp.arange(half, dtype=F32) / half)
    ang = pos.astype(F32)[:, None] * inv[None, :]
    return jnp.cos(ang), jnp.sin(ang)


def _trunk(x, pos, wts, bias, *, n_seq, seq_len, state=None):
    D = x.shape[-1]
    M = n_seq * seq_len
    x = x.reshape(M, D)
    sample = state is not None
    chunk = min(CHUNK, seq_len)
    cos, sin = _rope_tables(pos)

    n_main = wts['w_in'].shape[1] - GLA_GATE_RANK
    z, la = _norm_matmul(x, wts['norm_mix_g'][0], wts['w_in'], bm=MM_BM, bn=IN_BN, n_cols=n_main,
                         gate=(wts['w_in_lo'], wts['gate_w2'], wts['gate_b']))
    if sample:
        o, s_ret, s_gla = _ab_mixer(z, la, cos, sin, wts['ret_g'], wts['gla_g'], n_seq=n_seq, seq_len=seq_len,
                                    chunk=chunk, rows_per_step=AB_ROWS, s_ret=state['ret'], s_gla=state['gla'])
    else:
        o, s_ret, s_gla = _ab_mixer(z, la, cos, sin, wts['ret_g'], wts['gla_g'], n_seq=n_seq, seq_len=seq_len,
                                    chunk=chunk, rows_per_step=AB_ROWS)
    x = _proj_residual(x, o, wts['w_out_ab'], bm=PROJ_BM)
    hist0 = state['conv'][0] if sample else None
    x, tail0 = _conv_ffn(x, wts['norm_ffn_g'][0], wts['w_up'], wts['conv_w'][0], wts['conv_b'][0],
                         wts['w_down'], layer=0, bm=FFN_BM, bf=FFN_BF, seq_len=seq_len, hist=hist0)

    z = _norm_matmul(x, wts['norm_mix_g'][1], wts['w_qkv'], bm=MM_BM, bn=QKV_BN)
    if sample:
        o, new_k, new_v = _attn_sample(z, state['k'], state['v'], bias, n_seq=n_seq, seq_len=seq_len)
    else:
        o, new_k, new_v = _attn_prompt(z, bias, n_seq=n_seq, seq_len=seq_len)
    x = _proj_residual(x, o, wts['w_out_att'], bm=PROJ_BM)
    hist1 = state['conv'][1] if sample else None
    y, tail1 = _conv_ffn(x, wts['norm_ffn_g'][1], wts['w_up'], wts['conv_w'][1], wts['conv_b'][1],
                         wts['w_down'], layer=1, bm=FFN_BM, bf=FFN_BF, seq_len=seq_len, hist=hist1,
                         final_g=wts['norm_final_g'])

    def conv_state(tail):
        F = tail.shape[-1]
        t = tail.reshape(n_seq, -1, SUBLANES, F)[:, -1]
        return t[:, SUBLANES - (CONV_W - 1):]

    conv = jnp.stack([conv_state(tail0), conv_state(tail1)])
    kv_shape = (1, n_seq, -1, H_ATT, 128)
    return (y.reshape(n_seq, seq_len, D), s_ret[None], s_gla[None], new_k.reshape(kv_shape), new_v.reshape(kv_shape),
            conv)


def kernel(x_prompt, x_sample, state_ret, state_gla, cache_attn_k, cache_attn_v, state_ffn_conv, norm_mix_g, w_in_ab, gla_gate_w2, gla_gate_b, ret_norm_g, gla_norm_g, w_out_ab, w_qkv_att, rel_bias_att, w_out_att, norm_ffn_g, w_ffn_up, ffn_conv_w, ffn_conv_b, w_ffn_down, norm_final_g):
    B, T, D = x_prompt.shape
    SB, ST, _ = x_sample.shape
    main = w_in_ab.shape[2] - GLA_GATE_RANK
    wts = dict(
        norm_mix_g=norm_mix_g[:, None, :],
        w_in=w_in_ab[0].astype(BF16),
        w_in_lo=jnp.pad(w_in_ab[0, :, main:], ((0, 0), (0, LANES - GLA_GATE_RANK))).astype(BF16),
        gate_w2=jnp.pad(gla_gate_w2[0], ((0, LANES - GLA_GATE_RANK), (0, 0))).astype(BF16),
        gate_b=gla_gate_b[0][None, :],
        ret_g=ret_norm_g[0][None, :],
        gla_g=gla_norm_g[0][None, :],
        w_out_ab=w_out_ab[0].astype(BF16),
        w_qkv=w_qkv_att[0].astype(BF16),
        w_out_att=w_out_att[0].astype(BF16),
        norm_ffn_g=norm_ffn_g[:, None, :],
        w_up=w_ffn_up.astype(BF16),
        conv_w=ffn_conv_w,
        conv_b=ffn_conv_b[:, None, :],
        w_down=w_ffn_down.astype(BF16),
        norm_final_g=norm_final_g[None, :],
    )
    bias = _bias_tiles(rel_bias_att[0])
    state = dict(
        ret=state_ret[0], gla=state_gla[0],
        k=cache_attn_k.reshape(cache_attn_k.shape[:2] + (-1, cache_attn_k.shape[-1])),
        v=cache_attn_v.reshape(cache_attn_v.shape[:2] + (-1, cache_attn_v.shape[-1])),
        conv=jnp.pad(state_ffn_conv, ((0, 0), (0, 0), (SUBLANES - (CONV_W - 1), 0), (0, 0))),
    )
    outs_p = _trunk(x_prompt, jnp.arange(T), wts, bias, n_seq=B, seq_len=T)
    outs_s = _trunk(x_sample, PAST_LEN + jnp.arange(ST), wts, bias, n_seq=SB, seq_len=ST, state=state)
    return (outs_p[0], outs_s[0]) + outs_p[1:] + outs_s[1:]
```

```python
import functools
import math

import jax
import jax.numpy as jnp
import numpy as np
from jax import lax
from jax.experimental import pallas as pl
from jax.experimental.pallas import tpu as pltpu

F32 = jnp.float32
BF16 = jnp.bfloat16

CHUNK = 64
RMS_EPS = 1e-6
H_RET = 4
H_GLA = 4
GLA_GATE_RANK = 16
GLA_GATE_TAU = 16.0
ROPE_BASE = 10000.0
H_ATT = 16
LEFT_CHUNKS = 8
MAX_REL = 256
NEG_INF = -1e30
CONV_W = 3
PAST_LEN = 2048

LANES = 128
SUBLANES = 8
VMEM_LIMIT = 60 * 1024 * 1024

BAND = (LEFT_CHUNKS + 1) * CHUNK
Q_TILE = 2 * CHUNK
K_WIN = BAND + CHUNK
TOEP_W = 1024

MM_BM = 1024
IN_BN, QKV_BN = 1792, 2048
IN_BN_F32, QKV_BN_F32 = 896, 768
PROJ_BM = 1024
FFN_BM, FFN_BF = 1024, 512
FFN_BF_F32 = 256
AB_ROWS = 512
FFN_GROUP = 256
ATT_GROUP = 12
ATT_STRIP = 32


def _cparams(sem):
    return pltpu.CompilerParams(dimension_semantics=sem, vmem_limit_bytes=VMEM_LIMIT)


def _rms(x, g):
    return x * lax.rsqrt(jnp.mean(x * x, axis=-1, keepdims=True) + RMS_EPS) * g


def _silu(x):
    return x * jax.nn.sigmoid(x)


def _dot(a, b):
    return jnp.dot(a, b, preferred_element_type=F32)


def _dot_nt(a, b):
    return lax.dot_general(a, b, (((1,), (1,)), ((), ())), preferred_element_type=F32)


def _dot_tn(a, b):
    return lax.dot_general(a, b, (((0,), (0,)), ((), ())), preferred_element_type=F32)


def _norm_mm_kernel(*refs, has_gate, emit_w):
    refs = list(refs)
    x_ref, g_ref, w_ref = refs[:3]
    pos = 3
    if has_gate:
        wlo_ref, w2_ref, gb_ref = refs[pos:pos + 3]
        pos += 3
    o_ref = refs[pos]
    pos += 1
    if has_gate:
        la_ref = refs[pos]
        pos += 1
    if emit_w:
        wb_ref = refs[pos]
        pos += 1
    h_sc = refs[pos]

    @pl.when(pl.program_id(1) == 0)
    def _():
        h = _rms(x_ref[...], g_ref[...]).astype(BF16)
        h_sc[...] = h
        if has_gate:
            lo = _dot(h, wlo_ref[...])
            xg = _dot(lo.astype(BF16), w2_ref[...]) + gb_ref[...]
            la_ref[...] = jax.nn.log_sigmoid(xg) / GLA_GATE_TAU

    w = w_ref[...].astype(BF16)
    if emit_w:
        wb_ref[...] = w
    o_ref[...] = _dot(h_sc[...], w)


def _norm_matmul(x, g, w, *, bm, bn, gate=None, n_cols=None):
    M, D = x.shape
    N = w.shape[1] if n_cols is None else n_cols
    bm = min(bm, M)
    emit_w = w.dtype != BF16
    assert not emit_w or M == bm
    grid = (M // bm, N // bn)
    in_specs = [pl.BlockSpec((bm, D), lambda i, j: (i, 0)),
                pl.BlockSpec((1, D), lambda i, j: (0, 0)),
                pl.BlockSpec((D, bn), lambda i, j: (0, j))]
    args = [x, g, w]
    out_specs = [pl.BlockSpec((bm, bn), lambda i, j: (i, j))]
    out_shape = [jax.ShapeDtypeStruct((M, N), F32)]
    if gate is not None:
        wlo, w2, gb = gate
        G = w2.shape[1]
        in_specs += [pl.BlockSpec(wlo.shape, lambda i, j: (0, 0)),
                     pl.BlockSpec(w2.shape, lambda i, j: (0, 0)),
                     pl.BlockSpec((1, G), lambda i, j: (0, 0))]
        args += [wlo, w2, gb]
        out_specs.append(pl.BlockSpec((bm, G), lambda i, j: (i, 0)))
        out_shape.append(jax.ShapeDtypeStruct((M, G), F32))
    if emit_w:
        out_specs.append(pl.BlockSpec((D, bn), lambda i, j: (0, j)))
        out_shape.append(jax.ShapeDtypeStruct((D, N), BF16))
    kern = functools.partial(_norm_mm_kernel, has_gate=gate is not None, emit_w=emit_w)
    return pl.pallas_call(
        kern, grid=grid, in_specs=in_specs, out_specs=out_specs, out_shape=out_shape,
        scratch_shapes=[pltpu.VMEM((bm, D), BF16)],
        compiler_params=_cparams(("parallel", "arbitrary")), name="norm_matmul")(*args)


def _proj_res_kernel(x_ref, o_ref, w_ref, y_ref):
    y_ref[...] = x_ref[...] + _dot(o_ref[...], w_ref[...])


def _proj_residual(x, o, w, *, bm):
    M, D = x.shape
    K = o.shape[1]
    bm = min(bm, M)
    return pl.pallas_call(
        _proj_res_kernel, grid=(M // bm,),
        in_specs=[pl.BlockSpec((bm, D), lambda i: (i, 0)),
                  pl.BlockSpec((bm, K), lambda i: (i, 0)),
                  pl.BlockSpec((K, D), lambda i: (0, 0))],
        out_specs=pl.BlockSpec((bm, D), lambda i: (i, 0)),
        out_shape=jax.ShapeDtypeStruct((M, D), F32),
        compiler_params=_cparams(("parallel",)), name="proj_residual")(x, o, w)


def _ffn_kernel(*refs, nseq, seq_rows, tiles_per_seq, final_norm, emit_w):
    refs = list(refs)
    x_ref, g_ref, wg_ref, wu_ref, cw_ref, cb_ref, wd_ref = refs[:7]
    pos = 7
    hist_ref = None
    if tiles_per_seq is None:
        hist_ref = refs[pos]
        pos += 1
    gf_ref = None
    if final_norm:
        gf_ref = refs[pos]
        pos += 1
    y_ref, tail_ref = refs[pos:pos + 2]
    pos += 2
    if emit_w:
        wgb_ref, wub_ref, wdb_ref = refs[pos:pos + 3]
        pos += 3
    h_sc, gbuf_sc = refs[pos:pos + 2]
    carry_sc = refs[pos + 2] if tiles_per_seq is not None else None

    m = pl.program_id(0)
    f = pl.program_id(1)
    nf = pl.num_programs(1)
    L = seq_rows
    H = SUBLANES

    @pl.when(f == 0)
    def _():
        x = x_ref[...]
        h_sc[...] = _rms(x, g_ref[...]).astype(BF16)
        y_ref[...] = x
        if carry_sc is not None:
            @pl.when(m == 0)
            def _():
                carry_sc[...] = jnp.zeros(carry_sc.shape, F32)

    h = h_sc[...]
    bm = h.shape[0]
    bf = wg_ref.shape[1]
    n_split = bf // FFN_GROUP
    hw = FFN_GROUP
    wg = wg_ref[...].astype(BF16)
    wu = wu_ref[...].astype(BF16)
    wd = wd_ref[...].astype(BF16)
    if emit_w:
        wgb_ref[...] = wg
        wub_ref[...] = wu
        wdb_ref[...] = wd
    gates = [_dot(h, wg[:, c * hw:(c + 1) * hw]) for c in range(n_split)]
    ups = [_dot(h, wu[:, c * hw:(c + 1) * hw]) for c in range(n_split)]
    if tiles_per_seq is None:
        hist = hist_ref[...]
    else:
        hist = jnp.where((m % tiles_per_seq) == 0, 0.0, carry_sc[f])
    acc = None
    tails = []
    for c in range(n_split):
        cs = slice(c * hw, (c + 1) * hw)
        g3 = gates[c].reshape(nseq, L, hw)
        gbuf_sc[:, H:H + L, cs] = g3
        gbuf_sc[:, 0:H, cs] = hist[:, :, cs]
        tails.append(g3[:, L - H:L, :])
        g1 = gbuf_sc[:, H - 1:H - 1 + L, cs]
        g2 = gbuf_sc[:, H - 2:H - 2 + L, cs]
        gc = cb_ref[:, cs] + g2 * cw_ref[0:1, cs]
        gc = gc + g1 * cw_ref[1:2, cs]
        gc = gc + g3 * cw_ref[2:3, cs]
        act = (_silu(gc) * ups[c].reshape(nseq, L, hw)).reshape(bm, hw).astype(BF16)
        d = _dot(act, wd[cs, :])
        acc = d if acc is None else acc + d
    tail = jnp.concatenate(tails, axis=-1)
    tail_ref[...] = tail
    if carry_sc is not None:
        carry_sc[f] = tail
    y_ref[...] += acc

    if final_norm:
        @pl.when(f == nf - 1)
        def _():
            y_ref[...] = _rms(y_ref[...], gf_ref[...])


def _conv_ffn(x, g, weights, cw, cb, *, bm, bf, seq_len, hist=None, final_g=None):
    M, D = x.shape
    emit_w = weights[0] == 'f32'
    bm = min(bm, M)
    if emit_w:
        _, w_up, w_down, layer = weights
        F = w_down.shape[1]
        assert M == bm
    else:
        _, wg, wu, wd = weights
        F = wd.shape[0]
    nf = F // bf
    if seq_len >= bm:
        nseq, seq_rows, tiles_per_seq = 1, bm, seq_len // bm
        n_groups = M // bm
    else:
        nseq, seq_rows, tiles_per_seq = bm // seq_len, seq_len, None
        n_groups = M // seq_len
    up_spec = pl.BlockSpec((D, bf), lambda i, j: (0, j))
    down_spec = pl.BlockSpec((bf, D), lambda i, j: (j, 0))
    if emit_w:
        w_specs = [pl.BlockSpec((None, D, bf), lambda i, j: (layer, 0, j)),
                   pl.BlockSpec((None, D, bf), lambda i, j: (layer, 0, nf + j)),
                   pl.BlockSpec((None, bf, D), lambda i, j: (layer, j, 0))]
        w_args = [w_up, w_up, w_down]
    else:
        w_specs = [up_spec, up_spec, down_spec]
        w_args = [wg, wu, wd]
    in_specs = [pl.BlockSpec((bm, D), lambda i, j: (i, 0)),
                pl.BlockSpec((1, D), lambda i, j: (0, 0)),
                w_specs[0], w_specs[1],
                pl.BlockSpec((CONV_W, bf), lambda i, j: (0, j)),
                pl.BlockSpec((1, bf), lambda i, j: (0, j)),
                w_specs[2]]
    args = [x, g, w_args[0], w_args[1], cw, cb, w_args[2]]
    if tiles_per_seq is None:
        in_specs.append(pl.BlockSpec((nseq, SUBLANES, bf), lambda i, j: (i, 0, j)))
        args.append(hist)
    if final_g is not None:
        in_specs.append(pl.BlockSpec((1, D), lambda i, j: (0, 0)))
        args.append(final_g)
    scratch = [pltpu.VMEM((bm, D), BF16), pltpu.VMEM((nseq, SUBLANES + seq_rows, bf), F32)]
    if tiles_per_seq is not None:
        scratch.append(pltpu.VMEM((nf, 1, SUBLANES, bf), F32))
    kern = functools.partial(_ffn_kernel, nseq=nseq, seq_rows=seq_rows, tiles_per_seq=tiles_per_seq,
                             final_norm=final_g is not None, emit_w=emit_w)
    out_specs = [pl.BlockSpec((bm, D), lambda i, j: (i, 0)),
                 pl.BlockSpec((nseq, SUBLANES, bf), lambda i, j: (i, 0, j))]
    out_shape = [jax.ShapeDtypeStruct((M, D), F32), jax.ShapeDtypeStruct((n_groups, SUBLANES, F), F32)]
    if emit_w:
        out_specs += [up_spec, up_spec, down_spec]
        out_shape += [jax.ShapeDtypeStruct((D, F), BF16), jax.ShapeDtypeStruct((D, F), BF16),
                      jax.ShapeDtypeStruct((F, D), BF16)]
    return pl.pallas_call(
        kern, grid=(M // bm, nf), in_specs=in_specs, out_specs=out_specs, out_shape=out_shape,
        scratch_shapes=scratch,
        compiler_params=_cparams(("arbitrary", "arbitrary")), name="conv_ffn")(*args)


def _ab_kernel(*refs, C, n_sub, has_state):
    refs = list(refs)
    z_ref, la_ref, cos_ref, sin_ref, rg_ref, gg_ref = refs[:6]
    pos = 6
    if has_state:
        sr0_ref, sg0_ref = refs[pos:pos + 2]
        pos += 2
    o_ref, sr_out_ref, sg_out_ref = refs[pos:pos + 3]
    sret_sc, sgla_sc = refs[pos + 3:pos + 5]

    t = pl.program_id(1)
    nt = pl.num_programs(1)
    DK_R = 256
    DV = 256
    DK_G = 128

    @pl.when(t == 0)
    def _():
        if has_state:
            sret_sc[...] = sr0_ref[0]
            sgla_sc[...] = sg0_ref[0]
        else:
            sret_sc[...] = jnp.zeros(sret_sc.shape, F32)
            sgla_sc[...] = jnp.zeros(sgla_sc.shape, F32)

    row = lax.broadcasted_iota(jnp.int32, (C, C), 0)
    col = lax.broadcasted_iota(jnp.int32, (C, C), 1)
    causal = row >= col
    diff = jnp.where(causal, row - col, 0).astype(F32)
    tri = jnp.where(causal, 1.0, 0.0).astype(BF16)
    ridx = lax.broadcasted_iota(jnp.int32, (C, 1), 0).astype(F32)
    SB = SUBLANES
    nsb = C // SB
    lane_c = lax.broadcasted_iota(jnp.int32, (C, LANES), 1)
    row_c = lax.broadcasted_iota(jnp.int32, (C, LANES), 0)
    blk0_c = (row_c // SB) * SB
    ones = jnp.ones((LANES, LANES), BF16)
    rg = rg_ref[...]
    gg = gg_ref[...]

    def chunk(c, carry):
        r0 = pl.multiple_of(c * C, C)
        rows = pl.ds(r0, C)
        cos = cos_ref[rows, :]
        sin = sin_ref[rows, :]

        def rope(x):
            x1 = x[:, :LANES]
            x2 = x[:, LANES:]
            return jnp.concatenate([x1 * cos - x2 * sin, x1 * sin + x2 * cos], axis=-1)

        def ret_head(h):
            lg = math.log1p(-2.0 ** (-5.0 - h))
            q = rope(z_ref[rows, pl.ds(h * DK_R, DK_R)])
            k = rope(z_ref[rows, pl.ds(H_RET * DK_R + h * DK_R, DK_R)]) * (DK_R ** -0.5)
            v = z_ref[rows, pl.ds(2 * H_RET * DK_R + h * DV, DV)].astype(BF16)
            gate = z_ref[rows, pl.ds(2 * H_RET * DK_R + H_RET * DV + h * DV, DV)]
            S = sret_sc[h]
            qk = _dot_nt(q.astype(BF16), k.astype(BF16))
            q_dec = jnp.exp((ridx + 1.0) * lg)
            inter = _dot((q * q_dec).astype(BF16), S.astype(BF16))
            k_dec = jnp.exp((C - 1.0 - ridx) * lg)
            s_new = math.exp(C * lg) * S + _dot_tn((k * k_dec).astype(BF16), v)
            yield
            decay = jnp.where(causal, jnp.exp(diff * lg), 0.0)
            o = _dot((qk * decay).astype(BF16), v) + inter
            ret[h] = ((_rms(o, rg) * _silu(gate)).astype(o_ref.dtype), s_new)

        base = 2 * H_RET * DK_R + 2 * H_RET * DV

        def gla_head(h):
            q = z_ref[rows, pl.ds(base + h * DK_G, DK_G)] * (DK_G ** -0.5)
            k = z_ref[rows, pl.ds(base + H_GLA * DK_G + h * DK_G, DK_G)]
            v = z_ref[rows, pl.ds(base + 2 * H_GLA * DK_G + h * DV, DV)].astype(BF16)
            gate = z_ref[rows, pl.ds(base + 2 * H_GLA * DK_G + H_GLA * DV + h * DV, DV)]
            la = la_ref[rows, pl.ds(h * DK_G, DK_G)]
            la_hi = la.astype(BF16)
            la_lo = (la - la_hi.astype(F32)).astype(BF16)
            b2 = _dot(tri, jnp.concatenate([la_hi, la_lo], axis=-1))
            yield
            b = b2[:, :DK_G] + b2[:, DK_G:]
            S = sgla_sc[h]
            o = _dot((q * jnp.exp(b)).astype(BF16), S.astype(BF16))
            b_last = b[C - 1:C, :]
            kd = k * jnp.exp(b_last - b)
            e_col = jnp.transpose(jnp.broadcast_to(jnp.exp(b_last), (LANES, DK_G)))[:, 0:1]
            s_new = e_col * S + _dot_tn(kd.astype(BF16), v)
            q3 = q.reshape(nsb, SB, DK_G)
            k3 = k.reshape(nsb, SB, DK_G)
            b3 = b.reshape(nsb, SB, DK_G)
            pieces = []
            for j in range(SB):
                w = jnp.exp(jnp.minimum(b3 - b3[:, j:j + 1, :], 0.0))
                pieces.append((q3 * k3[:, j:j + 1, :] * w).reshape(C, DK_G))
            sums = _dot(jnp.concatenate(pieces, axis=0).astype(BF16), ones)
            left = []
            for I in range(1, nsb):
                i0 = I * SB
                b_r = b[i0:i0 + 1, :]
                qt = q[i0:i0 + SB, :] * jnp.exp(b[i0:i0 + SB, :] - b_r)
                before = row_c < i0
                kt = jnp.where(before, k * jnp.exp(jnp.where(before, b_r - b, 0.0)), 0.0)
                left.append(_dot_nt(qt.astype(BF16), kt.astype(BF16)))
            yield
            diag = jnp.zeros((C, LANES), F32)
            for j in range(SB):
                diag = jnp.where(lane_c == blk0_c + j, sums[j * C:(j + 1) * C, :], diag)
            diag = jnp.where(lane_c <= row_c, diag, 0.0)
            blocks = [diag[0:SB, :C]] + [left[I - 1] + diag[I * SB:(I + 1) * SB, :C] for I in range(1, nsb)]
            scores = jnp.concatenate(blocks, axis=0)
            o = o + _dot(scores.astype(BF16), v)
            gla[h] = ((_rms(o, gg) * _silu(gate)).astype(o_ref.dtype), s_new)

        assert H_RET == H_GLA
        ret, gla = [None] * H_RET, [None] * H_GLA
        heads = []
        for h in range(H_RET):
            heads += [ret_head(h), gla_head(h)]
        while heads:
            heads = [g for g in heads if next(g, True) is None]
        o_ref[rows, :] = jnp.concatenate([o for o, _ in ret] + [o for o, _ in gla], axis=-1)
        sret_sc[...] = jnp.stack([s for _, s in ret])
        sgla_sc[...] = jnp.stack([s for _, s in gla])
        return carry

    lax.fori_loop(0, n_sub, chunk, 0)

    @pl.when(t == nt - 1)
    def _():
        sr_out_ref[0] = sret_sc[...]
        sg_out_ref[0] = sgla_sc[...]


def _ab_mixer(z, la, cos, sin, rg, gg, *, n_seq, seq_len, chunk, rows_per_step, s_ret=None, s_gla=None):
    M, ZW = z.shape
    R = min(rows_per_step, seq_len)
    nt = seq_len // R
    has_state = s_ret is not None
    in_specs = [pl.BlockSpec((R, ZW), lambda b, t: (b * nt + t, 0)),
                pl.BlockSpec((R, la.shape[1]), lambda b, t: (b * nt + t, 0)),
                pl.BlockSpec((R, LANES), lambda b, t: (t, 0)),
                pl.BlockSpec((R, LANES), lambda b, t: (t, 0)),
                pl.BlockSpec((1, 256), lambda b, t: (0, 0)),
                pl.BlockSpec((1, 256), lambda b, t: (0, 0))]
    args = [z, la, cos, sin, rg, gg]
    sr_spec = pl.BlockSpec((1, H_RET, 256, 256), lambda b, t: (b, 0, 0, 0))
    sg_spec = pl.BlockSpec((1, H_GLA, 128, 256), lambda b, t: (b, 0, 0, 0))
    if has_state:
        in_specs += [sr_spec, sg_spec]
        args += [s_ret, s_gla]
    kern = functools.partial(_ab_kernel, C=chunk, n_sub=R // chunk, has_state=has_state)
    return pl.pallas_call(
        kern, grid=(n_seq, nt), in_specs=in_specs,
        out_specs=[pl.BlockSpec((R, 2048), lambda b, t: (b * nt + t, 0)), sr_spec, sg_spec],
        out_shape=[jax.ShapeDtypeStruct((M, 2048), BF16),
                   jax.ShapeDtypeStruct((n_seq, H_RET, 256, 256), F32),
                   jax.ShapeDtypeStruct((n_seq, H_GLA, 128, 256), F32)],
        scratch_shapes=[pltpu.VMEM((H_RET, 256, 256), F32), pltpu.VMEM((H_GLA, 128, 256), F32)],
        compiler_params=_cparams(("parallel", "arbitrary")), name="ab_mixer")(*args)


def _bias_kernel(rb_ref, o_ref):
    NB = rb_ref.shape[1]
    u = lax.broadcasted_iota(jnp.int32, (NB, TOEP_W), 1)
    d = jnp.where(u < TOEP_W - Q_TILE, u, u - TOEP_W)
    idx = jnp.clip(LEFT_CHUNKS * CHUNK - d, -MAX_REL, MAX_REL) + MAX_REL
    mrow = lax.broadcasted_iota(jnp.int32, (NB, TOEP_W), 0)
    onehot = jnp.where(mrow == idx, 1.0, 0.0).astype(BF16)
    rb = rb_ref[...]
    hi = rb.astype(BF16)
    r1 = rb - hi.astype(F32)
    mid = r1.astype(BF16)
    lo = (r1 - mid.astype(F32)).astype(BF16)
    grow = _dot(hi, onehot) + _dot(mid, onehot) + _dot(lo, onehot)
    i = lax.broadcasted_iota(jnp.int32, (Q_TILE, K_WIN), 0)
    j = lax.broadcasted_iota(jnp.int32, (Q_TILE, K_WIN), 1)
    rel_chunk = j // CHUNK - i // CHUNK
    in_band = (rel_chunk >= 0) & (rel_chunk <= LEFT_CHUNKS)
    for h in range(rb_ref.shape[0]):
        g = jnp.broadcast_to(grow[h:h + 1, :], (Q_TILE, TOEP_W))
        toep = pltpu.roll(g, 0, 1, stride=1, stride_axis=0)[:, :K_WIN]
        o_ref[h] = jnp.where(in_band, toep, NEG_INF)


def _bias_tiles(rel_bias):
    H, NR = rel_bias.shape
    NB = 5 * LANES
    rbp = jnp.pad(rel_bias, ((0, 0), (0, NB - NR)))
    return pl.pallas_call(
        _bias_kernel, out_shape=jax.ShapeDtypeStruct((H, Q_TILE, K_WIN), F32),
        compiler_params=pltpu.CompilerParams(vmem_limit_bytes=VMEM_LIMIT), name="rel_bias_tiles")(rbp)


def _attn_prompt_kernel(q_ref, k_ref, v_ref, bias_ref, o_ref, ko_ref, vo_ref, kpad_sc, vpad_sc):
    T, DH = k_ref.shape
    KEEP = ko_ref.shape[0] // H_ATT
    PAD = LEFT_CHUNKS * CHUNK
    h = pl.program_id(1)
    k = k_ref[...]
    v = v_ref[...]
    ko_ref[pl.ds(h, KEEP, stride=H_ATT), :] = k[T - KEEP:, :]
    vo_ref[pl.ds(h, KEEP, stride=H_ATT), :] = v[T - KEEP:, :]
    kpad_sc[0:PAD, :] = jnp.zeros((PAD, DH), BF16)
    vpad_sc[0:PAD, :] = jnp.zeros((PAD, DH), BF16)
    kpad_sc[PAD:PAD + T, :] = k.astype(BF16)
    vpad_sc[PAD:PAD + T, :] = v.astype(BF16)
    scale = DH ** -0.5
    colj = lax.broadcasted_iota(jnp.int32, (ATT_STRIP, K_WIN), 1)

    def scores(r0):
        q = (q_ref[pl.ds(r0, Q_TILE), :] * scale).astype(BF16)
        return _dot_nt(q, kpad_sc[pl.ds(r0, K_WIN), :])

    def attend(r0, s, masked):
        ps, dens = [], []
        for r in range(0, Q_TILE, ATT_STRIP):
            sr = s[r:r + ATT_STRIP, :] + bias_ref[0, r:r + ATT_STRIP, :]
            if masked:
                sr = jnp.where(colj + r0 >= PAD, sr, NEG_INF)
            mx = jnp.max(sr, axis=-1, keepdims=True)
            p = jnp.exp(sr - mx)
            dens.append(jnp.sum(p, axis=-1, keepdims=True))
            ps.append(p.astype(BF16))
        o = _dot(jnp.concatenate(ps, axis=0), vpad_sc[pl.ds(r0, K_WIN), :])
        o = [o[i * ATT_STRIP:(i + 1) * ATT_STRIP, :] / den for i, den in enumerate(dens)]
        return jnp.concatenate(o, axis=0).astype(o_ref.dtype)

    n_tiles = T // Q_TILE
    n_masked = min(PAD // Q_TILE, n_tiles)
    n_rest = n_tiles - n_masked
    G = ATT_GROUP if n_rest % ATT_GROUP == 0 else 1

    def group(r0, n, n_mask):
        s_next = scores(r0)
        outs = []
        for t in range(n):
            s_cur = s_next
            if t + 1 < n:
                s_next = scores(r0 + (t + 1) * Q_TILE)
            outs.append(attend(r0 + t * Q_TILE, s_cur, t < n_mask))
        o_ref[pl.ds(r0, n * Q_TILE), :] = jnp.concatenate(outs, axis=0)

    if G == n_rest:
        group(0, n_tiles, n_masked)
    else:
        group(0, n_masked, n_masked)

        def rest(g, carry):
            group(pl.multiple_of(n_masked * Q_TILE + g * (G * Q_TILE), Q_TILE), G, 0)
            return carry

        lax.fori_loop(0, n_rest // G, rest, 0)


def _attn_prompt(z, bias, *, n_seq, seq_len):
    M = z.shape[0]
    DH = 128
    T = seq_len
    keep = min(LEFT_CHUNKS * CHUNK, T)
    kv_spec = pl.BlockSpec((None, None, keep * H_ATT, DH), lambda b, h: (0, b, 0, 0))
    kv_shape = jax.ShapeDtypeStruct((1, n_seq, keep * H_ATT, DH), F32)
    return pl.pallas_call(
        _attn_prompt_kernel, grid=(n_seq, H_ATT),
        in_specs=[pl.BlockSpec((T, DH), lambda b, h: (b, h)),
                  pl.BlockSpec((T, DH), lambda b, h: (b, H_ATT + h)),
                  pl.BlockSpec((T, DH), lambda b, h: (b, 2 * H_ATT + h)),
                  pl.BlockSpec((1, Q_TILE, K_WIN), lambda b, h: (h, 0, 0))],
        out_specs=[pl.BlockSpec((T, DH), lambda b, h: (b, h)), kv_spec, kv_spec],
        out_shape=[jax.ShapeDtypeStruct((M, H_ATT * DH), BF16), kv_shape, kv_shape],
        scratch_shapes=[pltpu.VMEM((LEFT_CHUNKS * CHUNK + T, DH), BF16),
                        pltpu.VMEM((LEFT_CHUNKS * CHUNK + T, DH), BF16)],
        compiler_params=_cparams(("parallel", "arbitrary")), name="attn_prompt")(z, z, z, bias)


def _attn_sample_kernel(q_ref, k_ref, v_ref, kc_ref, vc_ref, bias_ref, o_ref, ko_ref, vo_ref):
    T = q_ref.shape[0]
    W = kc_ref.shape[0] // H_ATT
    DH = 128
    scale = DH ** -0.5
    def scores(h):
        cs = pl.ds(h * DH, DH)
        q = (q_ref[:, cs] * scale).astype(BF16)
        kn = k_ref[:, cs]
        ko_ref[pl.ds(h, T, stride=H_ATT), :] = kn
        kc = kc_ref[pl.ds(h, W, stride=H_ATT), :].astype(BF16)
        s1 = _dot_nt(q, kc) + bias_ref[h, :, 0:W]
        s2 = _dot_nt(q, kn.astype(BF16)) + bias_ref[h, :, W:W + T]
        return s1, s2

    def attend(h, s1, s2):
        cs = pl.ds(h * DH, DH)
        vn = v_ref[:, cs]
        vo_ref[pl.ds(h, T, stride=H_ATT), :] = vn
        vc = vc_ref[pl.ds(h, W, stride=H_ATT), :].astype(BF16)
        mx = jnp.maximum(jnp.max(s1, axis=-1, keepdims=True), jnp.max(s2, axis=-1, keepdims=True))
        p1 = jnp.exp(s1 - mx)
        p2 = jnp.exp(s2 - mx)
        den = jnp.sum(p1, axis=-1, keepdims=True) + jnp.sum(p2, axis=-1, keepdims=True)
        o = (_dot(p1.astype(BF16), vc) + _dot(p2.astype(BF16), vn.astype(BF16))) / den
        return o.astype(o_ref.dtype)

    s_next = scores(0)
    outs = []
    for h in range(H_ATT):
        s_cur = s_next
        if h + 1 < H_ATT:
            s_next = scores(h + 1)
        outs.append(attend(h, *s_cur))
    o_ref[...] = jnp.concatenate(outs, axis=-1)


def _attn_sample(z, kc, vc, bias, *, n_seq, seq_len):
    M = z.shape[0]
    T = seq_len
    DH = 128
    E = H_ATT * DH
    WH = kc.shape[2]
    cache_spec = pl.BlockSpec((None, None, WH, DH), lambda b: (0, b, 0, 0))
    new_spec = pl.BlockSpec((None, None, T * H_ATT, DH), lambda b: (0, b, 0, 0))
    new_shape = jax.ShapeDtypeStruct((1, n_seq, T * H_ATT, DH), F32)
    return pl.pallas_call(
        _attn_sample_kernel, grid=(n_seq,),
        in_specs=[pl.BlockSpec((T, E), lambda b: (b, 0)),
                  pl.BlockSpec((T, E), lambda b: (b, 1)),
                  pl.BlockSpec((T, E), lambda b: (b, 2)),
                  cache_spec, cache_spec,
                  pl.BlockSpec((H_ATT, T, K_WIN), lambda b: (0, 0, 0))],
        out_specs=[pl.BlockSpec((T, E), lambda b: (b, 0)), new_spec, new_spec],
        out_shape=[jax.ShapeDtypeStruct((M, E), BF16), new_shape, new_shape],
        compiler_params=_cparams(("parallel",)), name="attn_sample")(z, z, z, kc, vc, bias)


def _rope_tables(pos):
    half = LANES
    inv = ROPE_BASE ** (-jnp.arange(half, dtype=F32) / half)
    ang = pos.astype(F32)[:, None] * inv[None, :]
    return jnp.cos(ang), jnp.sin(ang)


def _trunk(x, pos, wts, bias, *, n_seq, seq_len, state=None, w16=None):
    D = x.shape[-1]
    M = n_seq * seq_len
    x = x.reshape(M, D)
    sample = state is not None
    emit = w16 is None
    chunk = min(CHUNK, seq_len)
    cos, sin = _rope_tables(pos)
    made = {}

    def ffn_weights(layer):
        if emit:
            return ('f32', wts['w_up'], wts['w_down'], layer)
        return ('bf16',) + w16['ffn'][layer]

    gate = (wts['w_in_lo'], wts['gate_w2'], wts['gate_b'])
    if emit:
        n_main = wts['w_in'].shape[1] - GLA_GATE_RANK
        z, la, made['w_in'] = _norm_matmul(x, wts['norm_mix_g'][0], wts['w_in'], bm=MM_BM, bn=IN_BN_F32,
                                           n_cols=n_main, gate=gate)
    else:
        z, la = _norm_matmul(x, wts['norm_mix_g'][0], w16['w_in'], bm=MM_BM, bn=IN_BN, gate=gate)
    if sample:
        o, s_ret, s_gla = _ab_mixer(z, la, cos, sin, wts['ret_g'], wts['gla_g'], n_seq=n_seq, seq_len=seq_len,
                                    chunk=chunk, rows_per_step=AB_ROWS, s_ret=state['ret'], s_gla=state['gla'])
    else:
        o, s_ret, s_gla = _ab_mixer(z, la, cos, sin, wts['ret_g'], wts['gla_g'], n_seq=n_seq, seq_len=seq_len,
                                    chunk=chunk, rows_per_step=AB_ROWS)
    x = _proj_residual(x, o, wts['w_out_ab'], bm=PROJ_BM)
    ffn_bf = FFN_BF_F32 if emit else FFN_BF
    made['ffn'] = []
    hist0 = state['conv'][0] if sample else None
    x, tail0, *w = _conv_ffn(x, wts['norm_ffn_g'][0], ffn_weights(0), wts['conv_w'][0], wts['conv_b'][0],
                             bm=FFN_BM, bf=ffn_bf, seq_len=seq_len, hist=hist0)
    made['ffn'].append(tuple(w))

    if emit:
        z, made['w_qkv'] = _norm_matmul(x, wts['norm_mix_g'][1], wts['w_qkv'], bm=MM_BM, bn=QKV_BN_F32)
    else:
        z, = _norm_matmul(x, wts['norm_mix_g'][1], w16['w_qkv'], bm=MM_BM, bn=QKV_BN)
    if sample:
        o, new_k, new_v = _attn_sample(z, state['k'], state['v'], bias, n_seq=n_seq, seq_len=seq_len)
    else:
        o, new_k, new_v = _attn_prompt(z, bias, n_seq=n_seq, seq_len=seq_len)
    x = _proj_residual(x, o, wts['w_out_att'], bm=PROJ_BM)
    hist1 = state['conv'][1] if sample else None
    y, tail1, *w = _conv_ffn(x, wts['norm_ffn_g'][1], ffn_weights(1), wts['conv_w'][1], wts['conv_b'][1],
                             bm=FFN_BM, bf=ffn_bf, seq_len=seq_len, hist=hist1, final_g=wts['norm_final_g'])
    made['ffn'].append(tuple(w))

    def conv_state(tail):
        F = tail.shape[-1]
        t = tail.reshape(n_seq, -1, SUBLANES, F)[:, -1]
        return t[:, SUBLANES - (CONV_W - 1):]

    conv = jnp.stack([conv_state(tail0), conv_state(tail1)])
    kv_shape = (1, n_seq, -1, H_ATT, 128)
    return (y.reshape(n_seq, seq_len, D), s_ret[None], s_gla[None], new_k.reshape(kv_shape), new_v.reshape(kv_shape),
            conv, made)


def kernel(x_prompt, x_sample, state_ret, state_gla, cache_attn_k, cache_attn_v, state_ffn_conv, norm_mix_g, w_in_ab, gla_gate_w2, gla_gate_b, ret_norm_g, gla_norm_g, w_out_ab, w_qkv_att, rel_bias_att, w_out_att, norm_ffn_g, w_ffn_up, ffn_conv_w, ffn_conv_b, w_ffn_down, norm_final_g):
    B, T, D = x_prompt.shape
    SB, ST, _ = x_sample.shape
    main = w_in_ab.shape[2] - GLA_GATE_RANK
    wts = dict(
        norm_mix_g=norm_mix_g[:, None, :],
        w_in=w_in_ab[0],
        w_in_lo=jnp.pad(w_in_ab[0, :, main:], ((0, 0), (0, LANES - GLA_GATE_RANK))).astype(BF16),
        gate_w2=jnp.pad(gla_gate_w2[0], ((0, LANES - GLA_GATE_RANK), (0, 0))).astype(BF16),
        gate_b=gla_gate_b[0][None, :],
        ret_g=ret_norm_g[0][None, :],
        gla_g=gla_norm_g[0][None, :],
        w_out_ab=w_out_ab[0].astype(BF16),
        w_qkv=w_qkv_att[0],
        w_out_att=w_out_att[0].astype(BF16),
        norm_ffn_g=norm_ffn_g[:, None, :],
        w_up=w_ffn_up,
        conv_w=ffn_conv_w,
        conv_b=ffn_conv_b[:, None, :],
        w_down=w_ffn_down,
        norm_final_g=norm_final_g[None, :],
    )
    bias = _bias_tiles(rel_bias_att[0])
    state = dict(
        ret=state_ret[0], gla=state_gla[0],
        k=cache_attn_k.reshape(cache_attn_k.shape[:2] + (-1, cache_attn_k.shape[-1])),
        v=cache_attn_v.reshape(cache_attn_v.shape[:2] + (-1, cache_attn_v.shape[-1])),
        conv=jnp.pad(state_ffn_conv, ((0, 0), (0, 0), (SUBLANES - (CONV_W - 1), 0), (0, 0))),
    )
    *outs_s, w16 = _trunk(x_sample, PAST_LEN + jnp.arange(ST), wts, bias, n_seq=SB, seq_len=ST, state=state)
    *outs_p, _ = _trunk(x_prompt, jnp.arange(T), wts, bias, n_seq=B, seq_len=T, w16=w16)
    return (outs_p[0], outs_s[0]) + tuple(outs_p[1:]) + tuple(outs_s[1:])
```

```python
import functools
import math

import jax
import jax.numpy as jnp
import numpy as np
from jax import lax
from jax.experimental import pallas as pl
from jax.experimental.pallas import tpu as pltpu

F32 = jnp.float32
BF16 = jnp.bfloat16

CHUNK = 64
RMS_EPS = 1e-6
H_RET = 4
H_GLA = 4
GLA_GATE_RANK = 16
GLA_GATE_TAU = 16.0
ROPE_BASE = 10000.0
H_ATT = 16
LEFT_CHUNKS = 8
MAX_REL = 256
NEG_INF = -1e30
CONV_W = 3
PAST_LEN = 2048

LANES = 128
SUBLANES = 8
VMEM_LIMIT = 60 * 1024 * 1024

BAND = (LEFT_CHUNKS + 1) * CHUNK
Q_TILE = 2 * CHUNK
K_WIN = BAND + CHUNK
TOEP_W = 1024

MM_BM = 1024
IN_BN, QKV_BN = 1792, 2048
IN_BN_F32, QKV_BN_F32 = 896, 768
PROJ_BM = 1024
FFN_BM, FFN_BF = 1024, 512
FFN_BF_F32 = 256
AB_ROWS = 512
FFN_GROUP = 256
ATT_GROUP = 12
ATT_HEADS = 2
ATT_STRIP = 32


def _cparams(sem):
    return pltpu.CompilerParams(dimension_semantics=sem, vmem_limit_bytes=VMEM_LIMIT)


def _rms(x, g):
    return x * lax.rsqrt(jnp.mean(x * x, axis=-1, keepdims=True) + RMS_EPS) * g


def _silu(x):
    return x * jax.nn.sigmoid(x)


def _dot(a, b):
    return jnp.dot(a, b, preferred_element_type=F32)


def _dot_nt(a, b):
    return lax.dot_general(a, b, (((1,), (1,)), ((), ())), preferred_element_type=F32)


def _dot_tn(a, b):
    return lax.dot_general(a, b, (((0,), (0,)), ((), ())), preferred_element_type=F32)


def _norm_mm_kernel(*refs, has_gate, emit_w, w_rows):
    refs = list(refs)
    x_ref, g_ref, w_ref = refs[:3]
    pos = 3
    if has_gate:
        wlo_ref, w2_ref, gb_ref = refs[pos:pos + 3]
        pos += 3
    o_ref = refs[pos]
    pos += 1
    if has_gate:
        la_ref = refs[pos]
        pos += 1
    if emit_w:
        wb_ref = refs[pos]
        pos += 1
    h_sc = refs[pos]

    @pl.when(pl.program_id(1) == 0)
    def _():
        h = _rms(x_ref[...], g_ref[...]).astype(BF16)
        h_sc[...] = h
        if has_gate:
            lo = _dot(h, wlo_ref[...])
            xg = _dot(lo.astype(BF16), w2_ref[...]) + gb_ref[...]
            la_ref[...] = (jnp.minimum(xg, 0.0) - jnp.log1p(jnp.exp(-jnp.abs(xg)))) / GLA_GATE_TAU

    w = w_ref[...].astype(BF16)
    if emit_w:
        wb_ref[...] = w
    o_ref[...] = _dot_nt(h_sc[...], w) if w_rows else _dot(h_sc[...], w)


def _norm_matmul(x, g, w, *, bm, bn, gate=None, n_cols=None, w_rows=False):
    M, D = x.shape
    N = w.shape[0 if w_rows else 1] if n_cols is None else n_cols
    bm = min(bm, M)
    emit_w = w.dtype != BF16
    assert not emit_w or M == bm
    grid = (M // bm, N // bn)
    w_spec = pl.BlockSpec((bn, D), lambda i, j: (j, 0)) if w_rows else pl.BlockSpec((D, bn), lambda i, j: (0, j))
    in_specs = [pl.BlockSpec((bm, D), lambda i, j: (i, 0)),
                pl.BlockSpec((1, D), lambda i, j: (0, 0)),
                w_spec]
    args = [x, g, w]
    out_specs = [pl.BlockSpec((bm, bn), lambda i, j: (i, j))]
    out_shape = [jax.ShapeDtypeStruct((M, N), F32)]
    if gate is not None:
        wlo, w2, gb = gate
        G = w2.shape[1]
        in_specs += [pl.BlockSpec(wlo.shape, lambda i, j: (0, 0)),
                     pl.BlockSpec(w2.shape, lambda i, j: (0, 0)),
                     pl.BlockSpec((1, G), lambda i, j: (0, 0))]
        args += [wlo, w2, gb]
        out_specs.append(pl.BlockSpec((bm, G), lambda i, j: (i, 0)))
        out_shape.append(jax.ShapeDtypeStruct((M, G), F32))
    if emit_w:
        out_specs.append(w_spec)
        out_shape.append(jax.ShapeDtypeStruct((N, D) if w_rows else (D, N), BF16))
    kern = functools.partial(_norm_mm_kernel, has_gate=gate is not None, emit_w=emit_w, w_rows=w_rows)
    return pl.pallas_call(
        kern, grid=grid, in_specs=in_specs, out_specs=out_specs, out_shape=out_shape,
        scratch_shapes=[pltpu.VMEM((bm, D), BF16)],
        compiler_params=_cparams(("parallel", "arbitrary")), name="norm_matmul")(*args)


def _proj_res_kernel(x_ref, o_ref, w_ref, y_ref):
    y_ref[...] = x_ref[...] + _dot(o_ref[...], w_ref[...])


def _proj_residual(x, o, w, *, bm):
    M, D = x.shape
    K = o.shape[1]
    bm = min(bm, M)
    return pl.pallas_call(
        _proj_res_kernel, grid=(M // bm,),
        in_specs=[pl.BlockSpec((bm, D), lambda i: (i, 0)),
                  pl.BlockSpec((bm, K), lambda i: (i, 0)),
                  pl.BlockSpec((K, D), lambda i: (0, 0))],
        out_specs=pl.BlockSpec((bm, D), lambda i: (i, 0)),
        out_shape=jax.ShapeDtypeStruct((M, D), F32),
        compiler_params=_cparams(("parallel",)), name="proj_residual")(x, o, w)


def _ffn_kernel(*refs, nseq, seq_rows, tiles_per_seq, final_norm, emit_w):
    refs = list(refs)
    x_ref, g_ref, wg_ref, wu_ref, cw_ref, cb_ref, wd_ref = refs[:7]
    pos = 7
    hist_ref = None
    if tiles_per_seq is None:
        hist_ref = refs[pos]
        pos += 1
    gf_ref = None
    if final_norm:
        gf_ref = refs[pos]
        pos += 1
    y_ref, tail_ref = refs[pos:pos + 2]
    pos += 2
    if emit_w:
        wgb_ref, wub_ref, wdb_ref = refs[pos:pos + 3]
        pos += 3
    h_sc, gbuf_sc = refs[pos:pos + 2]
    carry_sc = refs[pos + 2] if tiles_per_seq is not None else None

    m = pl.program_id(0)
    f = pl.program_id(1)
    nf = pl.num_programs(1)
    L = seq_rows
    H = SUBLANES

    @pl.when(f == 0)
    def _():
        x = x_ref[...]
        h_sc[...] = _rms(x, g_ref[...]).astype(BF16)
        y_ref[...] = x
        if carry_sc is not None:
            @pl.when(m == 0)
            def _():
                carry_sc[...] = jnp.zeros(carry_sc.shape, F32)

    h = h_sc[...]
    bm = h.shape[0]
    bf = wg_ref.shape[1]
    n_split = bf // FFN_GROUP
    hw = FFN_GROUP
    wg = wg_ref[...].astype(BF16)
    wu = wu_ref[...].astype(BF16)
    wd = wd_ref[...].astype(BF16)
    if emit_w:
        wgb_ref[...] = wg
        wub_ref[...] = wu
        wdb_ref[...] = wd
    gates = [_dot(h, wg[:, c * hw:(c + 1) * hw]) for c in range(n_split)]
    ups = [_dot(h, wu[:, c * hw:(c + 1) * hw]) for c in range(n_split)]
    if tiles_per_seq is None:
        hist = hist_ref[...]
    else:
        hist = jnp.where((m % tiles_per_seq) == 0, 0.0, carry_sc[f])
    acc = None
    tails = []
    for c in range(n_split):
        cs = slice(c * hw, (c + 1) * hw)
        g3 = gates[c].reshape(nseq, L, hw)
        gbuf_sc[:, H:H + L, cs] = g3
        gbuf_sc[:, 0:H, cs] = hist[:, :, cs]
        tails.append(g3[:, L - H:L, :])
        g1 = gbuf_sc[:, H - 1:H - 1 + L, cs]
        g2 = gbuf_sc[:, H - 2:H - 2 + L, cs]
        gc = cb_ref[:, cs] + g2 * cw_ref[0:1, cs]
        gc = gc + g1 * cw_ref[1:2, cs]
        gc = gc + g3 * cw_ref[2:3, cs]
        act = (_silu(gc) * ups[c].reshape(nseq, L, hw)).reshape(bm, hw).astype(BF16)
        d = _dot(act, wd[cs, :])
        acc = d if acc is None else acc + d
    tail = jnp.concatenate(tails, axis=-1)
    tail_ref[...] = tail
    if carry_sc is not None:
        carry_sc[f] = tail
    y_ref[...] += acc

    if final_norm:
        @pl.when(f == nf - 1)
        def _():
            y_ref[...] = _rms(y_ref[...], gf_ref[...])


def _conv_ffn(x, g, weights, cw, cb, *, bm, bf, seq_len, hist=None, final_g=None):
    M, D = x.shape
    emit_w = weights[0] == 'f32'
    bm = min(bm, M)
    if emit_w:
        _, w_up, w_down, layer = weights
        F = w_down.shape[1]
        assert M == bm
    else:
        _, wg, wu, wd = weights
        F = wd.shape[0]
    nf = F // bf
    if seq_len >= bm:
        nseq, seq_rows, tiles_per_seq = 1, bm, seq_len // bm
        n_groups = M // bm
    else:
        nseq, seq_rows, tiles_per_seq = bm // seq_len, seq_len, None
        n_groups = M // seq_len
    up_spec = pl.BlockSpec((D, bf), lambda i, j: (0, j))
    down_spec = pl.BlockSpec((bf, D), lambda i, j: (j, 0))
    if emit_w:
        w_specs = [pl.BlockSpec((None, D, bf), lambda i, j: (layer, 0, j)),
                   pl.BlockSpec((None, D, bf), lambda i, j: (layer, 0, nf + j)),
                   pl.BlockSpec((None, bf, D), lambda i, j: (layer, j, 0))]
        w_args = [w_up, w_up, w_down]
    else:
        w_specs = [up_spec, up_spec, down_spec]
        w_args = [wg, wu, wd]
    in_specs = [pl.BlockSpec((bm, D), lambda i, j: (i, 0)),
                pl.BlockSpec((1, D), lambda i, j: (0, 0)),
                w_specs[0], w_specs[1],
                pl.BlockSpec((CONV_W, bf), lambda i, j: (0, j)),
                pl.BlockSpec((1, bf), lambda i, j: (0, j)),
                w_specs[2]]
    args = [x, g, w_args[0], w_args[1], cw, cb, w_args[2]]
    if tiles_per_seq is None:
        in_specs.append(pl.BlockSpec((nseq, SUBLANES, bf), lambda i, j: (i, 0, j)))
        args.append(hist)
    if final_g is not None:
        in_specs.append(pl.BlockSpec((1, D), lambda i, j: (0, 0)))
        args.append(final_g)
    scratch = [pltpu.VMEM((bm, D), BF16), pltpu.VMEM((nseq, SUBLANES + seq_rows, bf), F32)]
    if tiles_per_seq is not None:
        scratch.append(pltpu.VMEM((nf, 1, SUBLANES, bf), F32))
    kern = functools.partial(_ffn_kernel, nseq=nseq, seq_rows=seq_rows, tiles_per_seq=tiles_per_seq,
                             final_norm=final_g is not None, emit_w=emit_w)
    out_specs = [pl.BlockSpec((bm, D), lambda i, j: (i, 0)),
                 pl.BlockSpec((nseq, SUBLANES, bf), lambda i, j: (i, 0, j))]
    out_shape = [jax.ShapeDtypeStruct((M, D), F32), jax.ShapeDtypeStruct((n_groups, SUBLANES, F), F32)]
    if emit_w:
        out_specs += [up_spec, up_spec, down_spec]
        out_shape += [jax.ShapeDtypeStruct((D, F), BF16), jax.ShapeDtypeStruct((D, F), BF16),
                      jax.ShapeDtypeStruct((F, D), BF16)]
    return pl.pallas_call(
        kern, grid=(M // bm, nf), in_specs=in_specs, out_specs=out_specs, out_shape=out_shape,
        scratch_shapes=scratch,
        compiler_params=_cparams(("arbitrary", "arbitrary")), name="conv_ffn")(*args)


def _ab_kernel(*refs, C, n_sub, has_state):
    refs = list(refs)
    z_ref, la_ref, cos_ref, sin_ref, rg_ref, gg_ref = refs[:6]
    pos = 6
    if has_state:
        sr0_ref, sg0_ref = refs[pos:pos + 2]
        pos += 2
    o_ref, sr_out_ref, sg_out_ref = refs[pos:pos + 3]
    sret_sc, sgla_sc = refs[pos + 3:pos + 5]

    t = pl.program_id(1)
    nt = pl.num_programs(1)
    DK_R = 256
    DV = 256
    DK_G = 128

    @pl.when(t == 0)
    def _():
        if has_state:
            sret_sc[...] = sr0_ref[0]
            sgla_sc[...] = sg0_ref[0]
        else:
            sret_sc[...] = jnp.zeros(sret_sc.shape, F32)
            sgla_sc[...] = jnp.zeros(sgla_sc.shape, F32)

    row = lax.broadcasted_iota(jnp.int32, (C, C), 0)
    col = lax.broadcasted_iota(jnp.int32, (C, C), 1)
    causal = row >= col
    diff = jnp.where(causal, row - col, 0).astype(F32)
    tri = jnp.where(causal, 1.0, 0.0).astype(BF16)
    ridx = lax.broadcasted_iota(jnp.int32, (C, 1), 0).astype(F32)
    SB = SUBLANES
    nsb = C // SB
    lane_c = lax.broadcasted_iota(jnp.int32, (C, LANES), 1)
    row_c = lax.broadcasted_iota(jnp.int32, (C, LANES), 0)
    blk0_c = (row_c // SB) * SB
    ones = jnp.ones((LANES, LANES), BF16)
    rg = rg_ref[...]
    gg = gg_ref[...]

    def chunk(c, carry):
        r0 = pl.multiple_of(c * C, C)
        rows = pl.ds(r0, C)
        cos = cos_ref[rows, :]
        sin = sin_ref[rows, :]

        def rope(x):
            x1 = x[:, :LANES]
            x2 = x[:, LANES:]
            return jnp.concatenate([x1 * cos - x2 * sin, x1 * sin + x2 * cos], axis=-1)

        def ret_head(h):
            lg = math.log1p(-2.0 ** (-5.0 - h))
            q = rope(z_ref[rows, pl.ds(h * DK_R, DK_R)])
            k = rope(z_ref[rows, pl.ds(H_RET * DK_R + h * DK_R, DK_R)]) * (DK_R ** -0.5)
            v = z_ref[rows, pl.ds(2 * H_RET * DK_R + h * DV, DV)].astype(BF16)
            gate = z_ref[rows, pl.ds(2 * H_RET * DK_R + H_RET * DV + h * DV, DV)]
            S = sret_sc[h]
            qk = _dot_nt(q.astype(BF16), k.astype(BF16))
            q_dec = jnp.exp((ridx + 1.0) * lg)
            inter = _dot((q * q_dec).astype(BF16), S.astype(BF16))
            k_dec = jnp.exp((C - 1.0 - ridx) * lg)
            s_new = math.exp(C * lg) * S + _dot_tn((k * k_dec).astype(BF16), v)
            yield
            decay = jnp.where(causal, jnp.exp(diff * lg), 0.0)
            o = _dot((qk * decay).astype(BF16), v) + inter
            ret[h] = ((_rms(o, rg) * _silu(gate)).astype(o_ref.dtype), s_new)

        base = 2 * H_RET * DK_R + 2 * H_RET * DV

        def gla_head(h):
            q = z_ref[rows, pl.ds(base + h * DK_G, DK_G)] * (DK_G ** -0.5)
            k = z_ref[rows, pl.ds(base + H_GLA * DK_G + h * DK_G, DK_G)]
            v = z_ref[rows, pl.ds(base + 2 * H_GLA * DK_G + h * DV, DV)].astype(BF16)
            gate = z_ref[rows, pl.ds(base + 2 * H_GLA * DK_G + H_GLA * DV + h * DV, DV)]
            la = la_ref[rows, pl.ds(h * DK_G, DK_G)]
            la_hi = la.astype(BF16)
            la_lo = (la - la_hi.astype(F32)).astype(BF16)
            b2 = _dot(tri, jnp.concatenate([la_hi, la_lo], axis=-1))
            yield
            b = b2[:, :DK_G] + b2[:, DK_G:]
            S = sgla_sc[h]
            o = _dot((q * jnp.exp(b)).astype(BF16), S.astype(BF16))
            b_last = b[C - 1:C, :]
            kd = k * jnp.exp(b_last - b)
            e_col = jnp.transpose(jnp.broadcast_to(jnp.exp(b_last), (LANES, DK_G)))[:, 0:1]
            s_new = e_col * S + _dot_tn(kd.astype(BF16), v)
            q3 = q.reshape(nsb, SB, DK_G)
            k3 = k.reshape(nsb, SB, DK_G)
            b3 = b.reshape(nsb, SB, DK_G)
            pieces = []
            for j in range(SB):
                w = jnp.exp(jnp.minimum(b3 - b3[:, j:j + 1, :], 0.0))
                pieces.append((q3 * k3[:, j:j + 1, :] * w).reshape(C, DK_G))
            sums = _dot(jnp.concatenate(pieces, axis=0).astype(BF16), ones)
            left = []
            for I in range(1, nsb):
                i0 = I * SB
                b_r = b[i0:i0 + 1, :]
                qt = q[i0:i0 + SB, :] * jnp.exp(b[i0:i0 + SB, :] - b_r)
                before = row_c < i0
                kt = jnp.where(before, k * jnp.exp(jnp.where(before, b_r - b, 0.0)), 0.0)
                left.append(_dot_nt(qt.astype(BF16), kt.astype(BF16)))
            yield
            diag = jnp.zeros((C, LANES), F32)
            for j in range(SB):
                diag = jnp.where(lane_c == blk0_c + j, sums[j * C:(j + 1) * C, :], diag)
            diag = jnp.where(lane_c <= row_c, diag, 0.0)
            blocks = [diag[0:SB, :C]] + [left[I - 1] + diag[I * SB:(I + 1) * SB, :C] for I in range(1, nsb)]
            scores = jnp.concatenate(blocks, axis=0)
            o = o + _dot(scores.astype(BF16), v)
            gla[h] = ((_rms(o, gg) * _silu(gate)).astype(o_ref.dtype), s_new)

        assert H_RET == H_GLA
        ret, gla = [None] * H_RET, [None] * H_GLA
        heads = []
        for h in range(H_RET):
            heads += [ret_head(h), gla_head(h)]
        while heads:
            heads = [g for g in heads if next(g, True) is None]
        o_ref[rows, :] = jnp.concatenate([o for o, _ in ret] + [o for o, _ in gla], axis=-1)
        sret_sc[...] = jnp.stack([s for _, s in ret])
        sgla_sc[...] = jnp.stack([s for _, s in gla])
        return carry

    lax.fori_loop(0, n_sub, chunk, 0)

    @pl.when(t == nt - 1)
    def _():
        sr_out_ref[0] = sret_sc[...]
        sg_out_ref[0] = sgla_sc[...]


def _ab_mixer(z, la, cos, sin, rg, gg, *, n_seq, seq_len, chunk, rows_per_step, s_ret=None, s_gla=None):
    M, ZW = z.shape
    R = min(rows_per_step, seq_len)
    nt = seq_len // R
    has_state = s_ret is not None
    in_specs = [pl.BlockSpec((R, ZW), lambda b, t: (b * nt + t, 0)),
                pl.BlockSpec((R, la.shape[1]), lambda b, t: (b * nt + t, 0)),
                pl.BlockSpec((R, LANES), lambda b, t: (t, 0)),
                pl.BlockSpec((R, LANES), lambda b, t: (t, 0)),
                pl.BlockSpec((1, 256), lambda b, t: (0, 0)),
                pl.BlockSpec((1, 256), lambda b, t: (0, 0))]
    args = [z, la, cos, sin, rg, gg]
    sr_spec = pl.BlockSpec((1, H_RET, 256, 256), lambda b, t: (b, 0, 0, 0))
    sg_spec = pl.BlockSpec((1, H_GLA, 128, 256), lambda b, t: (b, 0, 0, 0))
    if has_state:
        in_specs += [sr_spec, sg_spec]
        args += [s_ret, s_gla]
    kern = functools.partial(_ab_kernel, C=chunk, n_sub=R // chunk, has_state=has_state)
    return pl.pallas_call(
        kern, grid=(n_seq, nt), in_specs=in_specs,
        out_specs=[pl.BlockSpec((R, 2048), lambda b, t: (b * nt + t, 0)), sr_spec, sg_spec],
        out_shape=[jax.ShapeDtypeStruct((M, 2048), BF16),
                   jax.ShapeDtypeStruct((n_seq, H_RET, 256, 256), F32),
                   jax.ShapeDtypeStruct((n_seq, H_GLA, 128, 256), F32)],
        scratch_shapes=[pltpu.VMEM((H_RET, 256, 256), F32), pltpu.VMEM((H_GLA, 128, 256), F32)],
        compiler_params=_cparams(("parallel", "arbitrary")), name="ab_mixer")(*args)


def _bias_kernel(rb_ref, o_ref):
    NB = rb_ref.shape[1]
    u = lax.broadcasted_iota(jnp.int32, (NB, TOEP_W), 1)
    d = jnp.where(u < TOEP_W - Q_TILE, u, u - TOEP_W)
    idx = jnp.clip(LEFT_CHUNKS * CHUNK - d, -MAX_REL, MAX_REL) + MAX_REL
    mrow = lax.broadcasted_iota(jnp.int32, (NB, TOEP_W), 0)
    onehot = jnp.where(mrow == idx, 1.0, 0.0).astype(BF16)
    rb = rb_ref[...]
    hi = rb.astype(BF16)
    r1 = rb - hi.astype(F32)
    mid = r1.astype(BF16)
    lo = (r1 - mid.astype(F32)).astype(BF16)
    grow = _dot(hi, onehot) + _dot(mid, onehot) + _dot(lo, onehot)
    i = lax.broadcasted_iota(jnp.int32, (Q_TILE, K_WIN), 0)
    j = lax.broadcasted_iota(jnp.int32, (Q_TILE, K_WIN), 1)
    rel_chunk = j // CHUNK - i // CHUNK
    in_band = (rel_chunk >= 0) & (rel_chunk <= LEFT_CHUNKS)
    for h in range(rb_ref.shape[0]):
        g = jnp.broadcast_to(grow[h:h + 1, :], (Q_TILE, TOEP_W))
        toep = pltpu.roll(g, 0, 1, stride=1, stride_axis=0)[:, :K_WIN]
        o_ref[h] = jnp.where(in_band, toep, NEG_INF)


def _bias_tiles(rel_bias):
    H, NR = rel_bias.shape
    NB = 5 * LANES
    rbp = jnp.pad(rel_bias, ((0, 0), (0, NB - NR)))
    return pl.pallas_call(
        _bias_kernel, out_shape=jax.ShapeDtypeStruct((H, Q_TILE, K_WIN), F32),
        compiler_params=pltpu.CompilerParams(vmem_limit_bytes=VMEM_LIMIT), name="rel_bias_tiles")(rbp)


def _attn_prompt_kernel(q_ref, k_ref, v_ref, bias_ref, o_ref, ko_ref, vo_ref, kpad_sc, vpad_sc):
    T = k_ref.shape[0]
    DH = 128
    KEEP = ko_ref.shape[0] // H_ATT
    PAD = LEFT_CHUNKS * CHUNK
    scale = DH ** -0.5
    colj = lax.broadcasted_iota(jnp.int32, (ATT_STRIP, K_WIN), 1)
    n_tiles = T // Q_TILE
    n_masked = min(PAD // Q_TILE, n_tiles)
    n_rest = n_tiles - n_masked
    G = ATT_GROUP if n_rest % ATT_GROUP == 0 else 1

    def head(i):
        cs = pl.ds(i * DH, DH)
        h = pl.program_id(1) * ATT_HEADS + i
        k = k_ref[:, cs]
        v = v_ref[:, cs]
        ko_ref[pl.ds(h, KEEP, stride=H_ATT), :] = k[T - KEEP:, :]
        vo_ref[pl.ds(h, KEEP, stride=H_ATT), :] = v[T - KEEP:, :]
        kpad_sc[i, 0:PAD, :] = jnp.zeros((PAD, DH), BF16)
        vpad_sc[i, 0:PAD, :] = jnp.zeros((PAD, DH), BF16)
        kpad_sc[i, PAD:PAD + T, :] = k.astype(BF16)
        vpad_sc[i, PAD:PAD + T, :] = v.astype(BF16)

        def scores(r0):
            q = (q_ref[pl.ds(r0, Q_TILE), cs] * scale).astype(BF16)
            return _dot_nt(q, kpad_sc[i, pl.ds(r0, K_WIN), :])

        def attend(r0, s, masked):
            ps, dens = [], []
            for r in range(0, Q_TILE, ATT_STRIP):
                sr = s[r:r + ATT_STRIP, :] + bias_ref[i, r:r + ATT_STRIP, :]
                if masked:
                    sr = jnp.where(colj + r0 >= PAD, sr, NEG_INF)
                mx = jnp.max(sr, axis=-1, keepdims=True)
                p = jnp.exp(sr - mx)
                dens.append(jnp.sum(p, axis=-1, keepdims=True))
                ps.append(p.astype(BF16))
            o = _dot(jnp.concatenate(ps, axis=0), vpad_sc[i, pl.ds(r0, K_WIN), :])
            o = [o[j * ATT_STRIP:(j + 1) * ATT_STRIP, :] / den for j, den in enumerate(dens)]
            return jnp.concatenate(o, axis=0).astype(o_ref.dtype)

        def group(r0, n, n_mask):
            s_next = scores(r0)
            outs = []
            for t in range(n):
                s_cur = s_next
                if t + 1 < n:
                    s_next = scores(r0 + (t + 1) * Q_TILE)
                outs.append(attend(r0 + t * Q_TILE, s_cur, t < n_mask))
            o_ref[pl.ds(r0, n * Q_TILE), cs] = jnp.concatenate(outs, axis=0)

        if G == n_rest:
            group(0, n_tiles, n_masked)
        else:
            group(0, n_masked, n_masked)

            def rest(g, carry):
                group(pl.multiple_of(n_masked * Q_TILE + g * (G * Q_TILE), Q_TILE), G, 0)
                return carry

            lax.fori_loop(0, n_rest // G, rest, 0)

    for i in range(ATT_HEADS):
        head(i)


def _attn_prompt(z, bias, *, n_seq, seq_len):
    M = z.shape[0]
    DH = 128
    T = seq_len
    keep = min(LEFT_CHUNKS * CHUNK, T)
    kv_spec = pl.BlockSpec((None, None, keep * H_ATT, DH), lambda b, h: (0, b, 0, 0))
    kv_shape = jax.ShapeDtypeStruct((1, n_seq, keep * H_ATT, DH), F32)
    HP = ATT_HEADS
    NG = H_ATT // HP
    return pl.pallas_call(
        _attn_prompt_kernel, grid=(n_seq, NG),
        in_specs=[pl.BlockSpec((T, HP * DH), lambda b, h: (b, h)),
                  pl.BlockSpec((T, HP * DH), lambda b, h: (b, NG + h)),
                  pl.BlockSpec((T, HP * DH), lambda b, h: (b, 2 * NG + h)),
                  pl.BlockSpec((HP, Q_TILE, K_WIN), lambda b, h: (h, 0, 0))],
        out_specs=[pl.BlockSpec((T, HP * DH), lambda b, h: (b, h)), kv_spec, kv_spec],
        out_shape=[jax.ShapeDtypeStruct((M, H_ATT * DH), BF16), kv_shape, kv_shape],
        scratch_shapes=[pltpu.VMEM((HP, LEFT_CHUNKS * CHUNK + T, DH), BF16),
                        pltpu.VMEM((HP, LEFT_CHUNKS * CHUNK + T, DH), BF16)],
        compiler_params=_cparams(("parallel", "arbitrary")), name="attn_prompt")(z, z, z, bias)


def _attn_sample_kernel(q_ref, k_ref, v_ref, kc_ref, vc_ref, bias_ref, o_ref, ko_ref, vo_ref):
    T = q_ref.shape[0]
    W = kc_ref.shape[0] // H_ATT
    DH = 128
    scale = DH ** -0.5
    def scores(h):
        cs = pl.ds(h * DH, DH)
        q = (q_ref[:, cs] * scale).astype(BF16)
        kn = k_ref[:, cs]
        ko_ref[pl.ds(h, T, stride=H_ATT), :] = kn
        kc = kc_ref[pl.ds(h, W, stride=H_ATT), :].astype(BF16)
        s1 = _dot_nt(q, kc) + bias_ref[h, :, 0:W]
        s2 = _dot_nt(q, kn.astype(BF16)) + bias_ref[h, :, W:W + T]
        return s1, s2

    def attend(h, s1, s2):
        cs = pl.ds(h * DH, DH)
        vn = v_ref[:, cs]
        vo_ref[pl.ds(h, T, stride=H_ATT), :] = vn
        vc = vc_ref[pl.ds(h, W, stride=H_ATT), :].astype(BF16)
        mx = jnp.maximum(jnp.max(s1, axis=-1, keepdims=True), jnp.max(s2, axis=-1, keepdims=True))
        p1 = jnp.exp(s1 - mx)
        p2 = jnp.exp(s2 - mx)
        den = jnp.sum(p1, axis=-1, keepdims=True) + jnp.sum(p2, axis=-1, keepdims=True)
        o = (_dot(p1.astype(BF16), vc) + _dot(p2.astype(BF16), vn.astype(BF16))) / den
        return o.astype(o_ref.dtype)

    s_next = scores(0)
    outs = []
    for h in range(H_ATT):
        s_cur = s_next
        if h + 1 < H_ATT:
            s_next = scores(h + 1)
        outs.append(attend(h, *s_cur))
    o_ref[...] = jnp.concatenate(outs, axis=-1)


def _attn_sample(z, kc, vc, bias, *, n_seq, seq_len):
    M = z.shape[0]
    T = seq_len
    DH = 128
    E = H_ATT * DH
    WH = kc.shape[2]
    cache_spec = pl.BlockSpec((None, None, WH, DH), lambda b: (0, b, 0, 0))
    new_spec = pl.BlockSpec((None, None, T * H_ATT, DH), lambda b: (0, b, 0, 0))
    new_shape = jax.ShapeDtypeStruct((1, n_seq, T * H_ATT, DH), F32)
    return pl.pallas_call(
        _attn_sample_kernel, grid=(n_seq,),
        in_specs=[pl.BlockSpec((T, E), lambda b: (b, 0)),
                  pl.BlockSpec((T, E), lambda b: (b, 1)),
                  pl.BlockSpec((T, E), lambda b: (b, 2)),
                  cache_spec, cache_spec,
                  pl.BlockSpec((H_ATT, T, K_WIN), lambda b: (0, 0, 0))],
        out_specs=[pl.BlockSpec((T, E), lambda b: (b, 0)), new_spec, new_spec],
        out_shape=[jax.ShapeDtypeStruct((M, E), BF16), new_shape, new_shape],
        compiler_params=_cparams(("parallel",)), name="attn_sample")(z, z, z, kc, vc, bias)


def _rope_tables(pos):
    half = LANES
    inv = ROPE_BASE ** (-jnp.arange(half, dtype=F32) / half)
    ang = pos.astype(F32)[:, None] * inv[None, :]
    return jnp.cos(ang), jnp.sin(ang)


def _trunk(x, pos, wts, bias, *, n_seq, seq_len, state=None, w16=None):
    D = x.shape[-1]
    M = n_seq * seq_len
    x = x.reshape(M, D)
    sample = state is not None
    emit = w16 is None
    chunk = min(CHUNK, seq_len)
    cos, sin = _rope_tables(pos)
    made = {}

    def ffn_weights(layer):
        if emit:
            return ('f32', wts['w_up'], wts['w_down'], layer)
        return ('bf16',) + w16['ffn'][layer]

    gate = (wts['w_in_lo'], wts['gate_w2'], wts['gate_b'])
    if emit:
        n_main = wts['w_in_t'].shape[0] - GLA_GATE_RANK
        z, la, made['w_in_t'] = _norm_matmul(x, wts['norm_mix_g'][0], wts['w_in_t'], bm=MM_BM, bn=IN_BN_F32,
                                             n_cols=n_main, gate=gate, w_rows=True)
    else:
        z, la = _norm_matmul(x, wts['norm_mix_g'][0], w16['w_in_t'], bm=MM_BM, bn=IN_BN, gate=gate, w_rows=True)
    if sample:
        o, s_ret, s_gla = _ab_mixer(z, la, cos, sin, wts['ret_g'], wts['gla_g'], n_seq=n_seq, seq_len=seq_len,
                                    chunk=chunk, rows_per_step=AB_ROWS, s_ret=state['ret'], s_gla=state['gla'])
    else:
        o, s_ret, s_gla = _ab_mixer(z, la, cos, sin, wts['ret_g'], wts['gla_g'], n_seq=n_seq, seq_len=seq_len,
                                    chunk=chunk, rows_per_step=AB_ROWS)
    x = _proj_residual(x, o, wts['w_out_ab'], bm=PROJ_BM)
    ffn_bf = FFN_BF_F32 if emit else FFN_BF
    made['ffn'] = []
    hist0 = state['conv'][0] if sample else None
    x, tail0, *w = _conv_ffn(x, wts['norm_ffn_g'][0], ffn_weights(0), wts['conv_w'][0], wts['conv_b'][0],
                             bm=FFN_BM, bf=ffn_bf, seq_len=seq_len, hist=hist0)
    made['ffn'].append(tuple(w))

    if emit:
        z, made['w_qkv'] = _norm_matmul(x, wts['norm_mix_g'][1], wts['w_qkv'], bm=MM_BM, bn=QKV_BN_F32)
    else:
        z, = _norm_matmul(x, wts['norm_mix_g'][1], w16['w_qkv'], bm=MM_BM, bn=QKV_BN)
    if sample:
        o, new_k, new_v = _attn_sample(z, state['k'], state['v'], bias, n_seq=n_seq, seq_len=seq_len)
    else:
        o, new_k, new_v = _attn_prompt(z, bias, n_seq=n_seq, seq_len=seq_len)
    x = _proj_residual(x, o, wts['w_out_att'], bm=PROJ_BM)
    hist1 = state['conv'][1] if sample else None
    y, tail1, *w = _conv_ffn(x, wts['norm_ffn_g'][1], ffn_weights(1), wts['conv_w'][1], wts['conv_b'][1],
                             bm=FFN_BM, bf=ffn_bf, seq_len=seq_len, hist=hist1, final_g=wts['norm_final_g'])
    made['ffn'].append(tuple(w))

    def conv_state(tail):
        F = tail.shape[-1]
        t = tail.reshape(n_seq, -1, SUBLANES, F)[:, -1]
        return t[:, SUBLANES - (CONV_W - 1):]

    conv = jnp.stack([conv_state(tail0), conv_state(tail1)])
    kv_shape = (1, n_seq, -1, H_ATT, 128)
    return (y.reshape(n_seq, seq_len, D), s_ret[None], s_gla[None], new_k.reshape(kv_shape), new_v.reshape(kv_shape),
            conv, made)


def kernel(x_prompt, x_sample, state_ret, state_gla, cache_attn_k, cache_attn_v, state_ffn_conv, norm_mix_g, w_in_ab, gla_gate_w2, gla_gate_b, ret_norm_g, gla_norm_g, w_out_ab, w_qkv_att, rel_bias_att, w_out_att, norm_ffn_g, w_ffn_up, ffn_conv_w, ffn_conv_b, w_ffn_down, norm_final_g):
    B, T, D = x_prompt.shape
    SB, ST, _ = x_sample.shape
    main = w_in_ab.shape[2] - GLA_GATE_RANK
    wts = dict(
        norm_mix_g=norm_mix_g[:, None, :],
        w_in_t=jnp.swapaxes(w_in_ab[0], 0, 1),
        w_in_lo=jnp.pad(w_in_ab[0, :, main:], ((0, 0), (0, LANES - GLA_GATE_RANK))).astype(BF16),
        gate_w2=jnp.pad(gla_gate_w2[0], ((0, LANES - GLA_GATE_RANK), (0, 0))).astype(BF16),
        gate_b=gla_gate_b[0][None, :],
        ret_g=ret_norm_g[0][None, :],
        gla_g=gla_norm_g[0][None, :],
        w_out_ab=w_out_ab[0].astype(BF16),
        w_qkv=w_qkv_att[0],
        w_out_att=w_out_att[0].astype(BF16),
        norm_ffn_g=norm_ffn_g[:, None, :],
        w_up=w_ffn_up,
        conv_w=ffn_conv_w,
        conv_b=ffn_conv_b[:, None, :],
        w_down=w_ffn_down,
        norm_final_g=norm_final_g[None, :],
    )
    bias = _bias_tiles(rel_bias_att[0])
    state = dict(
        ret=state_ret[0], gla=state_gla[0],
        k=cache_attn_k.reshape(cache_attn_k.shape[:2] + (-1, cache_attn_k.shape[-1])),
        v=cache_attn_v.reshape(cache_attn_v.shape[:2] + (-1, cache_attn_v.shape[-1])),
        conv=jnp.pad(state_ffn_conv, ((0, 0), (0, 0), (SUBLANES - (CONV_W - 1), 0), (0, 0))),
    )
    *outs_s, w16 = _trunk(x_sample, PAST_LEN + jnp.arange(ST), wts, bias, n_seq=SB, seq_len=ST, state=state)
    *outs_p, _ = _trunk(x_prompt, jnp.arange(T), wts, bias, n_seq=B, seq_len=T, w16=w16)
    return (outs_p[0], outs_s[0]) + tuple(outs_p[1:]) + tuple(outs_s[1:])
```
